```python
import jax, jax.numpy as jnp
from jax import lax
import numpy as np

D_MODEL = 1024
BATCH = 8
SEQ = 8192
DEPTH = 4

D_MIX = D_MODEL
D_MLSTM = D_MIX // 2
D_CONV = D_MIX - D_MLSTM
N_MLSTM_HEADS = 4
HEAD_DIM = D_MLSTM // N_MLSTM_HEADS
CONV_WIDTH = 3
CHUNK = 128
N_EXPERTS = 16
N_GROUPS = 4
EXPERTS_PER_GROUP = N_EXPERTS // N_GROUPS
TOP_K = 2
D_FF_EXPERT = D_MODEL // 2
EXPERT_BLOCK = 128
LN_EPS = 1e-5
ALPHA = (2 * DEPTH) ** 0.25
BETA = (8 * DEPTH) ** -0.25

OFF_Q = 0
OFF_K = OFF_Q + D_MLSTM
OFF_V = OFF_K + D_MLSTM
OFF_O = OFF_V + D_MLSTM
OFF_I = OFF_O + D_MLSTM
OFF_F = OFF_I + N_MLSTM_HEADS
OFF_B = OFF_F + N_MLSTM_HEADS
OFF_C = OFF_B + D_CONV
OFF_X = OFF_C + D_CONV
P_IN = OFF_X + D_CONV

kernel_name = "hybrid_mlstm_shortconv_grouped_moe_deepnorm"


def layer_norm(x, g, b):
    xf = x.astype(jnp.float32)
    mu = xf.mean(-1, keepdims=True)
    var = jnp.square(xf - mu).mean(-1, keepdims=True)
    return ((xf - mu) * lax.rsqrt(var + LN_EPS) * g.astype(jnp.float32) + b.astype(jnp.float32)).astype(x.dtype)


def mlstm_cell(q, k, v, i_pre, f_pre):
    b, s, h, dh = q.shape
    nc = s // CHUNK
    f32 = jnp.float32

    def chunks(t):
        return t.astype(f32).reshape(b, nc, CHUNK, h, -1).transpose(0, 3, 1, 2, 4)

    q = chunks(q)
    k = chunks(k) * (dh ** -0.5)
    v = chunks(v)
    ig = chunks(i_pre[..., None])[..., 0]
    logf = jax.nn.log_sigmoid(chunks(f_pre[..., None])[..., 0])
    a = jnp.cumsum(logf, axis=-1)
    g = a[..., -1]

    w_end = g[..., None] - a + ig
    m_loc = w_end.max(-1)
    e_end = jnp.exp(w_end - m_loc[..., None])
    c_loc = jnp.einsum('bhcsk,bhcsv->bhckv', k * e_end[..., None], v)
    n_loc = jnp.einsum('bhcs,bhcsk->bhck', e_end, k)

    def step(carry, xs):
        c_prev, n_prev, m_prev = carry
        c_l, n_l, m_l, g_c = xs
        m_new = jnp.maximum(g_c + m_prev, m_l)
        s_prev = jnp.exp(g_c + m_prev - m_new)
        s_loc = jnp.exp(m_l - m_new)
        c_new = s_prev[..., None, None] * c_prev + s_loc[..., None, None] * c_l
        n_new = s_prev[..., None] * n_prev + s_loc[..., None] * n_l
        return (c_new, n_new, m_new), (c_prev, n_prev, m_prev)

    init = (jnp.zeros((b, h, dh, dh), f32), jnp.zeros((b, h, dh), f32), jnp.zeros((b, h), f32))
    xs = (jnp.moveaxis(c_loc, 2, 0), jnp.moveaxis(n_loc, 2, 0), jnp.moveaxis(m_loc, 2, 0), jnp.moveaxis(g, 2, 0))
    _, (c_st, n_st, m_st) = lax.scan(step, init, xs)
    c_st = jnp.moveaxis(c_st, 0, 2)
    n_st = jnp.moveaxis(n_st, 0, 2)
    m_st = jnp.moveaxis(m_st, 0, 2)

    causal = jnp.tril(jnp.ones((CHUNK, CHUNK), dtype=bool))
    log_d = jnp.where(causal, a[..., :, None] - a[..., None, :] + ig[..., None, :], -jnp.inf)
    log_inter = a + m_st[..., None]
    m_out = jnp.maximum(log_inter, log_d.max(-1))
    dmat = jnp.exp(log_d - m_out[..., None])
    s_qk = jnp.einsum('bhcjd,bhcsd->bhcjs', q, k) * dmat
    e_inter = jnp.exp(log_inter - m_out)
    num = jnp.einsum('bhcjs,bhcsv->bhcjv', s_qk, v) + e_inter[..., None] * jnp.einsum('bhcjk,bhckv->bhcjv', q, c_st)
    den = s_qk.sum(-1) + e_inter * jnp.einsum('bhcjk,bhck->bhcj', q, n_st)
    h_t = num / jnp.maximum(jnp.abs(den), jnp.exp(-m_out))[..., None]
    return h_t.transpose(0, 2, 3, 1, 4).reshape(b, s, h, dh)


def hybrid_mixer(x, w_in, b_in, conv_w, mh_g, w_out):
    b, s, _ = x.shape
    p = x @ w_in + b_in
    q = p[..., OFF_Q:OFF_K].reshape(b, s, N_MLSTM_HEADS, HEAD_DIM)
    k = p[..., OFF_K:OFF_V].reshape(b, s, N_MLSTM_HEADS, HEAD_DIM)
    v = p[..., OFF_V:OFF_O].reshape(b, s, N_MLSTM_HEADS, HEAD_DIM)
    o = p[..., OFF_O:OFF_I]
    i_pre = p[..., OFF_I:OFF_F]
    f_pre = p[..., OFF_F:OFF_B]
    h = mlstm_cell(q, k, v, i_pre, f_pre)
    mu = h.mean(-1, keepdims=True)
    var = jnp.square(h - mu).mean(-1, keepdims=True)
    h = (h - mu) * lax.rsqrt(var + LN_EPS) * mh_g.astype(jnp.float32).reshape(N_MLSTM_HEADS, HEAD_DIM)
    y_m = (jax.nn.sigmoid(o.astype(jnp.float32)) * h.reshape(b, s, D_MLSTM)).astype(x.dtype)

    gate_b = p[..., OFF_B:OFF_C]
    gate_c = p[..., OFF_C:OFF_X]
    xc = p[..., OFF_X:P_IN]
    u = gate_c * xc
    conv = lax.conv_general_dilated(u, conv_w[:, None, :].astype(u.dtype), window_strides=(1,),
                                    padding=[(CONV_WIDTH - 1, 0)], dimension_numbers=('NWC', 'WIO', 'NWC'),
                                    feature_group_count=D_CONV)
    y_c = gate_b * conv
    return jnp.concatenate([y_m, y_c], axis=-1) @ w_out


def grouped_moe(x, router_w, router_b, w_gate, w_up, w_down):
    b, s, d = x.shape
    n = b * s
    xf = x.reshape(n, d)
    probs = jax.nn.softmax((xf @ router_w).astype(jnp.float32) + router_b.astype(jnp.float32), axis=-1)
    pg = probs.reshape(n, N_GROUPS, EXPERTS_PER_GROUP)
    grp = jnp.argmax(lax.top_k(pg, TOP_K)[0].sum(-1), axis=-1)
    in_grp = jnp.take_along_axis(pg, grp[:, None, None], axis=1)[:, 0]
    top_p, top_local = lax.top_k(in_grp, TOP_K)
    expert_idx = grp[:, None] * EXPERTS_PER_GROUP + top_local
    gates = top_p / top_p.sum(-1, keepdims=True)

    flat_e = expert_idx.reshape(-1)
    order = jnp.argsort(flat_e)
    e_sorted = flat_e[order]
    tok_sorted = order // TOP_K
    gate_sorted = gates.reshape(-1)[order]
    counts = jnp.bincount(flat_e, length=N_EXPERTS)
    padded = (counts + EXPERT_BLOCK - 1) // EXPERT_BLOCK * EXPERT_BLOCK
    pad_end = jnp.cumsum(padded)
    pad_start = pad_end - padded
    start = jnp.cumsum(counts) - counts
    dest = pad_start[e_sorted] + (jnp.arange(n * TOP_K) - start[e_sorted])
    n_blocks = (n * TOP_K + EXPERT_BLOCK - 1) // EXPERT_BLOCK + N_EXPERTS
    buf = jnp.zeros((n_blocks * EXPERT_BLOCK, d), x.dtype).at[dest].set(xf[tok_sorted])
    block_expert = jnp.minimum(
        jnp.searchsorted(pad_end, jnp.arange(n_blocks) * EXPERT_BLOCK, side='right'), N_EXPERTS - 1)

    def expert_block(args):
        xb, e = args
        hdn = jax.nn.silu(xb @ w_gate[e]) * (xb @ w_up[e])
        return hdn @ w_down[e]

    y_buf = lax.map(expert_block, (buf.reshape(n_blocks, EXPERT_BLOCK, d), block_expert)).reshape(-1, d)
    y = y_buf[dest] * gate_sorted[:, None].astype(x.dtype)
    return jnp.zeros_like(xf).at[tok_sorted].add(y).reshape(b, s, d)


def setup_inputs(seed: int = 0) -> dict:
    key = jax.random.key(seed)
    ks = jax.random.split(key, 16)
    f32 = jnp.float32
    x = jax.random.normal(ks[0], (BATCH, SEQ, D_MODEL), f32)
    col_scale = jnp.ones((P_IN,), f32).at[OFF_V:OFF_O].set(BETA).at[OFF_X:P_IN].set(BETA)
    w_in = jax.random.normal(ks[1], (DEPTH, D_MODEL, P_IN), f32) * (D_MODEL ** -0.5) * col_scale
    b_in = (0.02 * jax.random.normal(ks[2], (DEPTH, P_IN), f32)).at[:, OFF_F:OFF_B].add(
        jnp.linspace(3.0, 6.0, N_MLSTM_HEADS, dtype=f32))
    conv_w = jax.random.normal(ks[3], (DEPTH, CONV_WIDTH, D_CONV), f32) * (CONV_WIDTH ** -0.5)
    mh_norm_g = 1.0 + 0.02 * jax.random.normal(ks[4], (DEPTH, D_MLSTM), f32)
    w_out = jax.random.normal(ks[5], (DEPTH, D_MIX, D_MODEL), f32) * (D_MIX ** -0.5) * BETA
    ln_mix_g = 1.0 + 0.02 * jax.random.normal(ks[6], (DEPTH, D_MODEL), f32)
    ln_mix_b = 0.02 * jax.random.normal(ks[7], (DEPTH, D_MODEL), f32)
    router_w = jax.random.normal(ks[8], (D_MODEL, N_EXPERTS), f32) * (D_MODEL ** -0.5)
    router_b = 0.01 * jax.random.normal(ks[9], (N_EXPERTS,), f32)
    w_gate = jax.random.normal(ks[10], (DEPTH, N_EXPERTS, D_MODEL, D_FF_EXPERT), f32) * (D_MODEL ** -0.5)
    w_up = jax.random.normal(ks[11], (DEPTH, N_EXPERTS, D_MODEL, D_FF_EXPERT), f32) * (D_MODEL ** -0.5)
    w_down = jax.random.normal(ks[12], (DEPTH, N_EXPERTS, D_FF_EXPERT, D_MODEL), f32) * (D_FF_EXPERT ** -0.5) * BETA
    ln_moe_g = 1.0 + 0.02 * jax.random.normal(ks[13], (DEPTH, D_MODEL), f32)
    ln_moe_b = 0.02 * jax.random.normal(ks[14], (DEPTH, D_MODEL), f32)
    return {"x": x, "w_in": w_in, "b_in": b_in, "conv_w": conv_w, "mh_norm_g": mh_norm_g, "w_out": w_out,
            "ln_mix_g": ln_mix_g, "ln_mix_b": ln_mix_b, "router_w": router_w, "router_b": router_b,
            "w_gate": w_gate, "w_up": w_up, "w_down": w_down, "ln_moe_g": ln_moe_g, "ln_moe_b": ln_moe_b}


def reference(x, w_in, b_in, conv_w, mh_norm_g, w_out, ln_mix_g, ln_mix_b, router_w, router_b,
              w_gate, w_up, w_down, ln_moe_g, ln_moe_b):
    for l in range(DEPTH):
        mix = hybrid_mixer(x, w_in[l], b_in[l], conv_w[l], mh_norm_g[l], w_out[l])
        x = layer_norm(ALPHA * x + mix, ln_mix_g[l], ln_mix_b[l])
        ffn = grouped_moe(x, router_w, router_b, w_gate[l], w_up[l], w_down[l])
        x = layer_norm(ALPHA * x + ffn, ln_moe_g[l], ln_moe_b[l])
    return x
```

```python
import functools

import jax
import jax.numpy as jnp
from jax import lax
from jax.experimental import pallas as pl
from jax.experimental.pallas import tpu as pltpu

D_MODEL = 1024
DEPTH = 4
D_MLSTM = D_MODEL // 2
D_CONV = D_MODEL - D_MLSTM
N_HEADS = 4
HEAD_DIM = D_MLSTM // N_HEADS
CONV_WIDTH = 3
CHUNK = 128
N_EXPERTS = 16
N_GROUPS = 4
EXPERTS_PER_GROUP = N_EXPERTS // N_GROUPS
TOP_K = 2
D_FF = D_MODEL // 2
LN_EPS = 1e-5
ALPHA = (2 * DEPTH) ** 0.25

OFF_Q = 0
OFF_K = OFF_Q + D_MLSTM
OFF_V = OFF_K + D_MLSTM
OFF_O = OFF_V + D_MLSTM
OFF_I = OFF_O + D_MLSTM
OFF_F = OFF_I + N_HEADS
OFF_B = OFF_F + N_HEADS
OFF_C = OFF_B + D_CONV
OFF_X = OFF_C + D_CONV
P_IN = OFF_X + D_CONV

LANES = 128
W_M = 4 * D_MLSTM
W_C = 3 * D_CONV
W_ALL = W_M + W_C + LANES

MIX_ROWS = 512
MOE_BLOCK = 256
ROW_TILE = 256
VMEM_LIMIT = 56 * 1024 * 1024

F32 = jnp.float32
BF16 = jnp.bfloat16


def _dot(a, b):
    return jnp.dot(a, b, preferred_element_type=F32)


def _dot_nt(a, b):
    return lax.dot_general(a, b, (((1,), (1,)), ((), ())), preferred_element_type=F32)


def _split_bf16(x):
    hi = x.astype(BF16)
    lo = (x - hi.astype(F32)).astype(BF16)
    return hi, lo


def _layer_norm(z, g, b):
    mu = jnp.mean(z, axis=-1, keepdims=True)
    d = z - mu
    var = jnp.mean(d * d, axis=-1, keepdims=True)
    return d * lax.rsqrt(var + LN_EPS) * g + b


def _mlstm_chunk(p_ref, y_ref, state_ref, m_ref, mhg_ref, tri, causal, r0):
    L = CHUNK
    pg = p_ref[pl.ds(r0, L), pl.ds(W_M + W_C, LANES)]
    logf = jnp.minimum(pg, 0.0) - jnp.log1p(jnp.exp(-jnp.abs(pg)))
    lf_hi, lf_lo = _split_bf16(logf)
    a_all = _dot(tri, lf_hi) + _dot(tri, lf_lo)
    pg_t = pg.T
    a_t = a_all.T
    one_col = (lax.broadcasted_iota(jnp.int32, (L, LANES), 1) == 0).astype(BF16)
    for h in range(N_HEADS):
        c0 = h * HEAD_DIM
        q = p_ref[pl.ds(r0, L), pl.ds(OFF_Q + c0, HEAD_DIM)]
        k = p_ref[pl.ds(r0, L), pl.ds(OFF_K + c0, HEAD_DIM)] * (HEAD_DIM ** -0.5)
        v = p_ref[pl.ds(r0, L), pl.ds(OFF_V + c0, HEAD_DIM)]
        o = p_ref[pl.ds(r0, L), pl.ds(OFF_O + c0, HEAD_DIM)]
        ig_col = pg[:, h:h + 1]
        a_col = a_all[:, N_HEADS + h:N_HEADS + h + 1]
        ig_row = pg_t[h:h + 1, :]
        a_row = a_t[N_HEADS + h:N_HEADS + h + 1, :]
        m_prev = m_ref[h:h + 1, 0:1]
        s_prev = state_ref[h]

        log_d = jnp.where(causal, a_col + (ig_row - a_row), -jnp.inf)
        log_inter = a_col + m_prev
        m_out = jnp.maximum(log_inter, jnp.max(log_d, axis=-1, keepdims=True))
        dmat = jnp.exp(log_d - m_out)
        qb = q.astype(BF16)
        kb = k.astype(BF16)
        v_aug = jnp.concatenate([v.astype(BF16), one_col], axis=1)
        s_qk = _dot_nt(qb, kb) * dmat
        tot = _dot(s_qk.astype(BF16), v_aug) + jnp.exp(log_inter - m_out) * _dot(qb, s_prev.astype(BF16))
        num = tot[:, :HEAD_DIM]
        den = tot[:, HEAD_DIM:HEAD_DIM + 1]
        hh = num / jnp.maximum(jnp.abs(den), jnp.exp(-m_out))

        mu = jnp.mean(hh, axis=-1, keepdims=True)
        dh = hh - mu
        var = jnp.mean(dh * dh, axis=-1, keepdims=True)
        hn = dh * lax.rsqrt(var + LN_EPS) * mhg_ref[:, pl.ds(c0, HEAD_DIM)]
        y_ref[pl.ds(r0, L), pl.ds(c0, HEAD_DIM)] = (jax.nn.sigmoid(o) * hn).astype(y_ref.dtype)

        g_tot = a_col[L - 1:L, :]
        w_end = g_tot - a_col + ig_col
        m_loc = jnp.max(w_end, axis=0, keepdims=True)
        k_e = (k * jnp.exp(w_end - m_loc)).T.astype(BF16)
        s_loc = _dot(k_e, v_aug)
        m_new = jnp.maximum(g_tot + m_prev, m_loc)
        state_ref[h] = jnp.exp(g_tot + m_prev - m_new) * s_prev + jnp.exp(m_loc - m_new) * s_loc
        m_ref[h:h + 1, :] = jnp.broadcast_to(m_new, (1, LANES))


def _route(logits_t, ts):
    lg = [logits_t[e:e + 1, :] for e in range(N_EXPERTS)]
    mx = functools.reduce(jnp.maximum, lg)
    ex = [jnp.exp(v - mx) for v in lg]
    tot = functools.reduce(lambda a, b: a + b, ex)
    p = [v / tot for v in ex]
    scores = []
    for g in range(N_GROUPS):
        p0, p1, p2, p3 = p[g * EXPERTS_PER_GROUP:(g + 1) * EXPERTS_PER_GROUP]
        hi01, lo01 = jnp.maximum(p0, p1), jnp.minimum(p0, p1)
        hi23, lo23 = jnp.maximum(p2, p3), jnp.minimum(p2, p3)
        top1 = jnp.maximum(hi01, hi23)
        top2 = jnp.maximum(jnp.minimum(hi01, hi23), jnp.maximum(lo01, lo23))
        scores.append(top1 + top2)
    grp = jnp.zeros((1, ts), jnp.int32)
    best = scores[0]
    for g in range(1, N_GROUPS):
        upd = scores[g] > best
        best = jnp.where(upd, scores[g], best)
        grp = jnp.where(upd, g, grp)
    sel = []
    for j in range(EXPERTS_PER_GROUP):
        v = p[j]
        for g in range(1, N_GROUPS):
            v = jnp.where(grp == g, p[g * EXPERTS_PER_GROUP + j], v)
        sel.append(v)
    i1 = jnp.zeros((1, ts), jnp.int32)
    v1 = sel[0]
    for j in range(1, EXPERTS_PER_GROUP):
        upd = sel[j] > v1
        v1 = jnp.where(upd, sel[j], v1)
        i1 = jnp.where(upd, j, i1)
    i2 = jnp.zeros((1, ts), jnp.int32)
    v2 = jnp.full((1, ts), -jnp.inf, F32)
    for j in range(EXPERTS_PER_GROUP):
        cand = jnp.where(i1 == j, -jnp.inf, sel[j])
        upd = cand > v2
        v2 = jnp.where(upd, cand, v2)
        i2 = jnp.where(upd, j, i2)
    e0 = grp * EXPERTS_PER_GROUP + i1
    e1 = grp * EXPERTS_PER_GROUP + i2
    s12 = v1 + v2
    return e0, e1, v1 / s12, v2 / s12


def _mixer_kernel(x_ref, w_ref, b_ref, cw_ref, mhg_ref, wo_ref, lng_ref, lnb_ref,
                  rwh_ref, rwl_ref, rb_ref,
                  xo_ref, eidx_ref, gate_ref, rank_ref, cnt_ref,
                  p_ref, y_ref, state_ref, m_ref, ucarry_ref, cnt_scr, *, ts):
    b_id = pl.program_id(0)
    s_id = pl.program_id(1)

    @pl.when(s_id == 0)
    def _():
        state_ref[...] = jnp.zeros_like(state_ref)
        m_ref[...] = jnp.zeros_like(m_ref)
        ucarry_ref[...] = jnp.zeros_like(ucarry_ref)

    @pl.when(jnp.logical_and(b_id == 0, s_id == 0))
    def _():
        cnt_scr[...] = jnp.zeros_like(cnt_scr)

    x = x_ref[0]
    p_ref[...] = _dot(x.astype(BF16), w_ref[...]) + b_ref[...]

    L = CHUNK
    row_i = lax.broadcasted_iota(jnp.int32, (L, L), 0)
    col_i = lax.broadcasted_iota(jnp.int32, (L, L), 1)
    causal = col_i <= row_i
    tri = causal.astype(BF16)
    for c in range(ts // L):
        _mlstm_chunk(p_ref, y_ref, state_ref, m_ref, mhg_ref, tri, causal, c * L)

    gate_b = p_ref[:, pl.ds(W_M, D_CONV)]
    u = p_ref[:, pl.ds(W_M + D_CONV, D_CONV)] * p_ref[:, pl.ds(W_M + 2 * D_CONV, D_CONV)]
    rid = lax.broadcasted_iota(jnp.int32, (ts, D_CONV), 0)
    prev1 = ucarry_ref[7:8, :]
    prev2 = ucarry_ref[6:7, :]
    u1 = jnp.where(rid == 0, prev1, pltpu.roll(u, 1, 0))
    u2 = jnp.where(rid == 0, prev2, jnp.where(rid == 1, prev1, pltpu.roll(u, 2, 0)))
    conv = cw_ref[0:1, :] * u2 + cw_ref[1:2, :] * u1 + cw_ref[2:3, :] * u
    y_ref[:, pl.ds(D_MLSTM, D_CONV)] = (gate_b * conv).astype(y_ref.dtype)
    ucarry_ref[...] = u[ts - 8:ts, :]

    mix = _dot(y_ref[...], wo_ref[...])
    x1 = _layer_norm(ALPHA * x + mix, lng_ref[...], lnb_ref[...])
    xo_ref[0] = x1

    x_hi, x_lo = _split_bf16(x1)
    logits_t = (_dot_nt(rwh_ref[...], x_hi) + _dot_nt(rwl_ref[...], x_hi)
                + _dot_nt(rwh_ref[...], x_lo) + rb_ref[...])
    e0, e1, g0, g1 = _route(logits_t, ts)
    eidx_ref[0:1, :] = e0
    eidx_ref[1:2, :] = e1
    gate_ref[0:1, :] = g0
    gate_ref[1:2, :] = g1

    eid = lax.broadcasted_iota(jnp.int32, (N_EXPERTS, ts), 0)
    hit0 = eid == e0
    hit1 = eid == e1
    onehot = hit0.astype(F32) + hit1.astype(F32)
    t_r = lax.broadcasted_iota(jnp.int32, (ts, ts), 0)
    t_c = lax.broadcasted_iota(jnp.int32, (ts, ts), 1)
    upper = (t_r <= t_c).astype(BF16)
    csum = _dot(onehot.astype(BF16), upper)
    before = csum - onehot + cnt_scr[:, 0:1]
    rank_ref[0:1, :] = jnp.sum(jnp.where(hit0, before, 0.0), axis=0, keepdims=True).astype(jnp.int32)
    rank_ref[1:2, :] = jnp.sum(jnp.where(hit1, before, 0.0), axis=0, keepdims=True).astype(jnp.int32)
    cnt_scr[...] = cnt_scr[...] + csum[:, ts - 1:ts]
    cnt_ref[...] = cnt_scr[...].astype(jnp.int32)


def _const_spec(shape):
    nd = len(shape)
    return pl.BlockSpec(shape, lambda *_: (0,) * nd, pipeline_mode=pl.Buffered(1))


def _mixer_call(x, w_all, b_all, conv_w, mh_g, w_out, ln_g, ln_b, rw_hi, rw_lo, rb):
    bsz, seq, d = x.shape
    ts = min(MIX_ROWS, seq)
    n = bsz * seq
    grid = (bsz, seq // ts)
    tok_spec = pl.BlockSpec((TOP_K, ts), lambda b, s: (0, b * (seq // ts) + s))
    return pl.pallas_call(
        functools.partial(_mixer_kernel, ts=ts),
        grid=grid,
        in_specs=[
            pl.BlockSpec((1, ts, d), lambda b, s: (b, s, 0)),
            _const_spec(w_all.shape), _const_spec(b_all.shape), _const_spec(conv_w.shape),
            _const_spec(mh_g.shape), _const_spec(w_out.shape), _const_spec(ln_g.shape),
            _const_spec(ln_b.shape), _const_spec(rw_hi.shape), _const_spec(rw_lo.shape),
            _const_spec(rb.shape),
        ],
        out_specs=[
            pl.BlockSpec((1, ts, d), lambda b, s: (b, s, 0)),
            tok_spec, tok_spec, tok_spec,
            pl.BlockSpec((N_EXPERTS, LANES), lambda b, s: (0, 0)),
        ],
        out_shape=[
            jax.ShapeDtypeStruct((bsz, seq, d), F32),
            jax.ShapeDtypeStruct((TOP_K, n), jnp.int32),
            jax.ShapeDtypeStruct((TOP_K, n), F32),
            jax.ShapeDtypeStruct((TOP_K, n), jnp.int32),
            jax.ShapeDtypeStruct((N_EXPERTS, LANES), jnp.int32),
        ],
        scratch_shapes=[
            pltpu.VMEM((ts, W_ALL), F32),
            pltpu.VMEM((ts, D_MODEL), BF16),
            pltpu.VMEM((N_HEADS, HEAD_DIM, 2 * HEAD_DIM), F32),
            pltpu.VMEM((8, LANES), F32),
            pltpu.VMEM((8, D_CONV), F32),
            pltpu.VMEM((N_EXPERTS, LANES), F32),
        ],
        compiler_params=pltpu.CompilerParams(
            dimension_semantics=("arbitrary", "arbitrary"), vmem_limit_bytes=VMEM_LIMIT),
        name="mixer",
    )(x, w_all, b_all, conv_w, mh_g, w_out, ln_g, ln_b, rw_hi, rw_lo, rb)


def _dispatch_kernel(pe_ref, dest_ref, x_ref, buf_ref, zero_ref, sem, zsem, *, rows):
    @pl.when(pl.program_id(0) == 0)
    def _():
        zero_ref[...] = jnp.zeros_like(zero_ref)

        def fill(start):
            start = pl.multiple_of(start, MOE_BLOCK)
            return pltpu.make_async_copy(zero_ref, buf_ref.at[pl.ds(start, MOE_BLOCK)], zsem)

        used = pe_ref[N_EXPERTS - 1]
        fills = [(pe_ref[e] - MOE_BLOCK, pe_ref[e] > (pe_ref[e - 1] if e else 0))
                 for e in range(N_EXPERTS)]
        fills += [(used + t * MOE_BLOCK, used + t * MOE_BLOCK < buf_ref.shape[0])
                  for t in range(N_EXPERTS)]
        for start, pred in fills:
            pl.when(pred)(lambda start=start: fill(start).start())
        for start, pred in fills:
            pl.when(pred)(lambda start=start: fill(start).wait())

    def issue(i, carry):
        for k in range(TOP_K):
            pltpu.make_async_copy(
                x_ref.at[pl.ds(i, 1)], buf_ref.at[pl.ds(dest_ref[k, i], 1)], sem.at[k]).start()
        return carry

    lax.fori_loop(0, rows, issue, 0, unroll=8)
    for k in range(TOP_K):
        pltpu.make_async_copy(x_ref, buf_ref.at[pl.ds(0, rows)], sem.at[k]).wait()


def _dispatch_call(x_flat, dest, pad_end, n_slots):
    n, d = x_flat.shape
    rows = min(ROW_TILE, n)
    return pl.pallas_call(
        functools.partial(_dispatch_kernel, rows=rows),
        grid_spec=pltpu.PrefetchScalarGridSpec(
            num_scalar_prefetch=1,
            grid=(n // rows,),
            in_specs=[
                pl.BlockSpec((TOP_K, rows), lambda i, pe: (0, i), memory_space=pltpu.SMEM),
                pl.BlockSpec((rows, d), lambda i, pe: (i, 0)),
            ],
            out_specs=pl.BlockSpec(memory_space=pl.ANY),
            scratch_shapes=[
                pltpu.VMEM((MOE_BLOCK, d), x_flat.dtype),
                pltpu.SemaphoreType.DMA((TOP_K,)),
                pltpu.SemaphoreType.DMA(()),
            ],
        ),
        out_shape=jax.ShapeDtypeStruct((n_slots, d), x_flat.dtype),
        compiler_params=pltpu.CompilerParams(
            dimension_semantics=("arbitrary",), has_side_effects=True),
        name="dispatch",
    )(pad_end, dest, x_flat)


def _expert_kernel(be_ref, nv_ref, buf_ref, wgu_ref, wd_ref, y_ref):
    used = pl.program_id(0) < nv_ref[0]

    @pl.when(used)
    def _():
        h = _dot(buf_ref[...].astype(BF16), wgu_ref[0])
        hdn = jax.nn.silu(h[:, :D_FF]) * h[:, D_FF:]
        y_ref[...] = _dot(hdn.astype(BF16), wd_ref[0])

    @pl.when(jnp.logical_not(used))
    def _():
        y_ref[...] = jnp.zeros_like(y_ref)


def _expert_call(block_expert, n_valid, buf, wgu, wd):
    n_slots, d = buf.shape
    n_blocks = n_slots // MOE_BLOCK

    def row_map(i, be, nv):
        return (jnp.minimum(i, nv[0] - 1), 0)

    def w_map(i, be, nv):
        return (be[jnp.minimum(i, nv[0] - 1)], 0, 0)

    return pl.pallas_call(
        _expert_kernel,
        grid_spec=pltpu.PrefetchScalarGridSpec(
            num_scalar_prefetch=2,
            grid=(n_blocks,),
            in_specs=[
                pl.BlockSpec((MOE_BLOCK, d), row_map),
                pl.BlockSpec((1, d, 2 * D_FF), w_map),
                pl.BlockSpec((1, D_FF, d), w_map),
            ],
            out_specs=pl.BlockSpec((MOE_BLOCK, d), lambda i, be, nv: (i, 0)),
        ),
        out_shape=jax.ShapeDtypeStruct((n_slots, d), F32),
        compiler_params=pltpu.CompilerParams(
            dimension_semantics=("arbitrary",), vmem_limit_bytes=VMEM_LIMIT),
        name="experts",
    )(block_expert, n_valid, buf, wgu, wd)


def _combine_kernel(dest_ref, x_ref, gate_ref, lng_ref, lnb_ref, y_hbm, o_ref, ybuf, sem, *, rows):
    def issue(i, carry):
        for k in range(TOP_K):
            pltpu.make_async_copy(
                y_hbm.at[pl.ds(dest_ref[k, i], 1)], ybuf.at[k, pl.ds(i, 1)], sem.at[k]).start()
        return carry

    lax.fori_loop(0, rows, issue, 0, unroll=8)
    for k in range(TOP_K):
        pltpu.make_async_copy(y_hbm.at[pl.ds(0, rows)], ybuf.at[k], sem.at[k]).wait()
    ffn =ybuf[0] * gate_ref[:, 0:1] + ybuf[1] * gate_ref[:, 1:2]
    o_ref[...] = _layer_norm(ALPHA * x_ref[...] + ffn, lng_ref[...], lnb_ref[...])


def _combine_call(x_flat, dest, gates_t, y_buf, ln_g, ln_b):
    n, d = x_flat.shape
    rows = min(ROW_TILE, n)
    return pl.pallas_call(
        functools.partial(_combine_kernel, rows=rows),
        grid=(n // rows,),
        in_specs=[
            pl.BlockSpec((TOP_K, rows), lambda i: (0, i), memory_space=pltpu.SMEM),
            pl.BlockSpec((rows, d), lambda i: (i, 0)),
            pl.BlockSpec((rows, TOP_K), lambda i: (i, 0)),
            _const_spec(ln_g.shape), _const_spec(ln_b.shape),
            pl.BlockSpec(memory_space=pl.ANY),
        ],
        out_specs=pl.BlockSpec((rows, d), lambda i: (i, 0)),
        out_shape=jax.ShapeDtypeStruct((n, d), F32),
        scratch_shapes=[pltpu.VMEM((TOP_K, rows, d), F32), pltpu.SemaphoreType.DMA((TOP_K,))],
        compiler_params=pltpu.CompilerParams(dimension_semantics=("arbitrary",)),
        name="combine",
    )(dest, x_flat, gates_t, ln_g, ln_b, y_buf)


def _slot_plan(counts, eidx, rank, n_blocks):
    padded = (counts + MOE_BLOCK - 1) // MOE_BLOCK * MOE_BLOCK
    pad_end = jnp.cumsum(padded)
    pad_start = pad_end - padded
    dest = pad_start[eidx] + rank
    block_expert = jnp.minimum(
        jnp.searchsorted(pad_end, jnp.arange(n_blocks, dtype=jnp.int32) * MOE_BLOCK, side="right"),
        N_EXPERTS - 1).astype(jnp.int32)
    n_valid = (pad_end[-1:] // MOE_BLOCK).astype(jnp.int32)
    return dest.astype(jnp.int32), pad_end.astype(jnp.int32), block_expert, n_valid


def kernel(x, w_in, b_in, conv_w, mh_norm_g, w_out, ln_mix_g, ln_mix_b, router_w, router_b,
           w_gate, w_up, w_down, ln_moe_g, ln_moe_b):
    bsz, seq, d = x.shape
    n = bsz * seq
    n_slots = n * TOP_K + N_EXPERTS * MOE_BLOCK
    n_blocks = n_slots // MOE_BLOCK

    gate_pad = jnp.zeros(w_in.shape[:2] + (LANES - 2 * N_HEADS,), w_in.dtype)
    w_all = jnp.concatenate(
        [w_in[..., OFF_Q:OFF_I], w_in[..., OFF_B:P_IN], w_in[..., OFF_I:OFF_B], gate_pad], axis=-1).astype(BF16)
    b_all = jnp.concatenate(
        [b_in[..., OFF_Q:OFF_I], b_in[..., OFF_B:P_IN], b_in[..., OFF_I:OFF_B], gate_pad[:, 0, :]],
        axis=-1).astype(F32)[:, None, :]
    w_out_b = w_out.astype(BF16)
    rw_t = router_w.T.astype(F32)
    rw_hi, rw_lo = _split_bf16(rw_t)
    rb = router_b.astype(F32)[:, None]
    wgu = jnp.concatenate([w_gate, w_up], axis=-1).astype(BF16)
    wd = w_down.astype(BF16)

    for l in range(DEPTH):
        x1, eidx, gates, rank, cnt = _mixer_call(
            x, w_all[l], b_all[l], conv_w[l], mh_norm_g[l][None, :], w_out_b[l],
            ln_mix_g[l][None, :], ln_mix_b[l][None, :], rw_hi, rw_lo, rb)
        dest, pad_end, block_expert, n_valid = _slot_plan(cnt[:, 0], eidx, rank, n_blocks)
        x1f = x1.reshape(n, d)
        buf = _dispatch_call(x1f, dest, pad_end, n_slots)
        y_buf = _expert_call(block_expert, n_valid, buf, wgu[l], wd[l])
        x = _combine_call(x1f, dest, gates.T, y_buf, ln_moe_g[l][None, :], ln_moe_b[l][None, :]).reshape(bsz, seq, d)
    return x
```

```python
import functools

import jax
import jax.numpy as jnp
from jax import lax
from jax.experimental import pallas as pl
from jax.experimental.pallas import tpu as pltpu

D_MODEL = 1024
DEPTH = 4
D_MLSTM = D_MODEL // 2
D_CONV = D_MODEL - D_MLSTM
N_HEADS = 4
HEAD_DIM = D_MLSTM // N_HEADS
CONV_WIDTH = 3
CHUNK = 128
N_EXPERTS = 16
N_GROUPS = 4
EXPERTS_PER_GROUP = N_EXPERTS // N_GROUPS
TOP_K = 2
D_FF = D_MODEL // 2
LN_EPS = 1e-5
ALPHA = (2 * DEPTH) ** 0.25

OFF_Q = 0
OFF_K = OFF_Q + D_MLSTM
OFF_V = OFF_K + D_MLSTM
OFF_O = OFF_V + D_MLSTM
OFF_I = OFF_O + D_MLSTM
OFF_F = OFF_I + N_HEADS
OFF_B = OFF_F + N_HEADS
OFF_C = OFF_B + D_CONV
OFF_X = OFF_C + D_CONV
P_IN = OFF_X + D_CONV

LANES = 128
W_M = 4 * D_MLSTM
W_C = 3 * D_CONV
GATE_I = W_M + W_C
GATE_F = GATE_I + LANES
W_ALL = GATE_F + LANES

MIX_ROWS = 256
MOE_BLOCK = 512
ROW_TILE = 256
VMEM_LIMIT = 56 * 1024 * 1024

F32 = jnp.float32
BF16 = jnp.bfloat16


def _dot(a, b):
    return jnp.dot(a, b, preferred_element_type=F32)


def _dot_nt(a, b):
    return lax.dot_general(a, b, (((1,), (1,)), ((), ())), preferred_element_type=F32)


def _split_bf16(x):
    hi = x.astype(BF16)
    lo = (x - hi.astype(F32)).astype(BF16)
    return hi, lo


def _layer_norm(z, g, b):
    mu = jnp.mean(z, axis=-1, keepdims=True)
    d = z - mu
    var = jnp.mean(d * d, axis=-1, keepdims=True)
    return d * lax.rsqrt(var + LN_EPS) * g + b


def _rows(x, c):
    return x[c * CHUNK:(c + 1) * CHUNK]


def _stack(parts):
    return parts[0] if len(parts) == 1 else jnp.concatenate(parts, axis=0)


def _mlstm_tile(p_ref, y_ref, state_ref, m_ref, mhg_ref, fresh, ts):
    L = CHUNK
    nch = ts // L
    row_i = lax.broadcasted_iota(jnp.int32, (L, L), 0)
    col_i = lax.broadcasted_iota(jnp.int32, (L, L), 1)
    tri = (col_i <= row_i).astype(BF16)
    causal = _stack([col_i <= row_i] * nch)
    one_col = (lax.broadcasted_iota(jnp.int32, (ts, LANES), 1) == 0).astype(BF16)

    ig = p_ref[:, pl.ds(GATE_I, LANES)]
    fg = p_ref[:, pl.ds(GATE_F, LANES)]
    logf = jnp.minimum(fg, 0.0) - jnp.log1p(jnp.exp(-jnp.abs(fg)))
    lf_hi, lf_lo = _split_bf16(logf)
    a = _stack([_dot(tri, _rows(lf_hi, c)) + _dot(tri, _rows(lf_lo, c)) for c in range(nch)])
    g = [a[(c + 1) * L - 1:(c + 1) * L, :] for c in range(nch)]
    w_end = _stack([g[c] - _rows(a, c) + _rows(ig, c) for c in range(nch)])
    m_loc = [jnp.max(_rows(w_end, c), axis=0, keepdims=True) for c in range(nch)]
    m = [jnp.where(fresh, 0.0, m_ref[0:1, :])]
    for c in range(nch):
        m.append(jnp.maximum(g[c] + m[c], m_loc[c]))
    m_ref[0:1, :] = m[nch]
    log_inter = a + _stack([jnp.broadcast_to(m[c], (L, LANES)) for c in range(nch)])
    e_end = jnp.exp(w_end - _stack([jnp.broadcast_to(m_loc[c], (L, LANES)) for c in range(nch)]))
    keep = [jnp.exp(g[c] + m[c] - m[c + 1]) for c in range(nch)]
    take = [jnp.exp(m_loc[c] - m[c + 1]) for c in range(nch)]
    b_t = [(_rows(ig, c) - _rows(a, c)).T for c in range(nch)]

    for h in range(N_HEADS):
        c0 = h * HEAD_DIM
        q = p_ref[:, pl.ds(OFF_Q + c0, HEAD_DIM)]
        k = p_ref[:, pl.ds(OFF_K + c0, HEAD_DIM)] * (HEAD_DIM ** -0.5)
        v = p_ref[:, pl.ds(OFF_V + c0, HEAD_DIM)]
        o = p_ref[:, pl.ds(OFF_O + c0, HEAD_DIM)]

        b_rows = _stack([jnp.broadcast_to(b_t[c][h:h + 1, :], (L, L)) for c in range(nch)])
        log_d = jnp.where(causal, a[:, h:h + 1] + b_rows, -jnp.inf)
        m_out = jnp.maximum(log_inter[:, h:h + 1], jnp.max(log_d, axis=-1, keepdims=True))
        dmat = jnp.exp(log_d - m_out)
        qb = q.astype(BF16)
        kb = k.astype(BF16)
        v_aug = jnp.concatenate([v.astype(BF16), one_col], axis=1)
        s_qk = (_stack([_dot_nt(_rows(qb, c), _rows(kb, c)) for c in range(nch)]) * dmat).astype(BF16)
        intra = _stack([_dot(_rows(s_qk, c), _rows(v_aug, c)) for c in range(nch)])
        k_e = k * e_end[:, h:h + 1]
        s_loc = [_dot(_rows(k_e, c).T.astype(BF16), _rows(v_aug, c)) for c in range(nch)]
        state = [jnp.where(fresh, 0.0, state_ref[h])]
        for c in range(nch):
            state.append(keep[c][:, h:h + 1] * state[c] + take[c][:, h:h + 1] * s_loc[c])
        state_ref[h] = state[nch]
        inter = _stack([_dot(_rows(qb, c), state[c].astype(BF16)) for c in range(nch)])
        tot = intra + jnp.exp(log_inter[:, h:h + 1] - m_out) * inter
        num = tot[:, :HEAD_DIM]
        den = tot[:, HEAD_DIM:HEAD_DIM + 1]
        hh = num / jnp.maximum(jnp.abs(den), jnp.exp(-m_out))

        mu = jnp.mean(hh, axis=-1, keepdims=True)
        dh = hh - mu
        var = jnp.mean(dh * dh, axis=-1, keepdims=True)
        hn = dh * lax.rsqrt(var + LN_EPS) * mhg_ref[:, pl.ds(c0, HEAD_DIM)]
        y_ref[:, pl.ds(c0, HEAD_DIM)] = (jax.nn.sigmoid(o) * hn).astype(y_ref.dtype)


def _route(logits_t, ts):
    lg = [logits_t[e:e + 1, :] for e in range(N_EXPERTS)]
    mx = functools.reduce(jnp.maximum, lg)
    ex = [jnp.exp(v - mx) for v in lg]
    tot = functools.reduce(lambda a, b: a + b, ex)
    p = [v / tot for v in ex]
    scores = []
    for g in range(N_GROUPS):
        p0, p1, p2, p3 = p[g * EXPERTS_PER_GROUP:(g + 1) * EXPERTS_PER_GROUP]
        hi01, lo01 = jnp.maximum(p0, p1), jnp.minimum(p0, p1)
        hi23, lo23 = jnp.maximum(p2, p3), jnp.minimum(p2, p3)
        top1 = jnp.maximum(hi01, hi23)
        top2 = jnp.maximum(jnp.minimum(hi01, hi23), jnp.maximum(lo01, lo23))
        scores.append(top1 + top2)
    grp = jnp.zeros((1, ts), jnp.int32)
    best = scores[0]
    for g in range(1, N_GROUPS):
        upd = scores[g] > best
        best = jnp.where(upd, scores[g], best)
        grp = jnp.where(upd, g, grp)
    sel = []
    for j in range(EXPERTS_PER_GROUP):
        v = p[j]
        for g in range(1, N_GROUPS):
            v = jnp.where(grp == g, p[g * EXPERTS_PER_GROUP + j], v)
        sel.append(v)
    i1 = jnp.zeros((1, ts), jnp.int32)
    v1 = sel[0]
    for j in range(1, EXPERTS_PER_GROUP):
        upd = sel[j] > v1
        v1 = jnp.where(upd, sel[j], v1)
        i1 = jnp.where(upd, j, i1)
    i2 = jnp.zeros((1, ts), jnp.int32)
    v2 = jnp.full((1, ts), -jnp.inf, F32)
    for j in range(EXPERTS_PER_GROUP):
        cand = jnp.where(i1 == j, -jnp.inf, sel[j])
        upd = cand > v2
        v2 = jnp.where(upd, cand, v2)
        i2 = jnp.where(upd, j, i2)
    e0 = grp * EXPERTS_PER_GROUP + i1
    e1 = grp * EXPERTS_PER_GROUP + i2
    s12 = v1 + v2
    return e0, e1, v1 / s12, v2 / s12


def _in_proj(x, w_ref, b_ref, p_ref):
    p_ref[...] = _dot(x.astype(BF16), w_ref[...]) + b_ref[...]


def _mix_tile(p_ref, x, r0, fresh, first_tile, refs, ts):
    (cw_ref, mhg_ref, wo_ref, lng_ref, lnb_ref, rwh_ref, rwl_ref, rb_ref,
     xo_ref, eidx_ref, gate_ref, rank_ref, cnt_ref,
     y_ref, state_ref, m_ref, ucarry_ref, cnt_scr) = refs
    _mlstm_tile(p_ref, y_ref, state_ref, m_ref, mhg_ref, fresh, ts)

    gate_b = p_ref[:, pl.ds(W_M, D_CONV)]
    u = p_ref[:, pl.ds(W_M + D_CONV, D_CONV)] * p_ref[:, pl.ds(W_M + 2 * D_CONV, D_CONV)]
    rid = lax.broadcasted_iota(jnp.int32, (ts, D_CONV), 0)
    prev1 = jnp.where(fresh, 0.0, ucarry_ref[7:8, :])
    prev2 = jnp.where(fresh, 0.0, ucarry_ref[6:7, :])
    u1 = jnp.where(rid == 0, prev1, pltpu.roll(u, 1, 0))
    u2 = jnp.where(rid == 0, prev2, jnp.where(rid == 1, prev1, pltpu.roll(u, 2, 0)))
    conv = cw_ref[0:1, :] * u2 + cw_ref[1:2, :] * u1 + cw_ref[2:3, :] * u
    y_ref[:, pl.ds(D_MLSTM, D_CONV)] = (gate_b * conv).astype(y_ref.dtype)
    ucarry_ref[...] = u[ts - 8:ts, :]

    mix = _dot(y_ref[...], wo_ref[...])
    x1 = _layer_norm(ALPHA * x + mix, lng_ref[...], lnb_ref[...])
    xo_ref[pl.ds(r0, ts), :] = x1

    x_hi, x_lo = _split_bf16(x1)
    logits_t = (_dot_nt(rwh_ref[...], x_hi) + _dot_nt(rwl_ref[...], x_hi)
                + _dot_nt(rwh_ref[...], x_lo) + rb_ref[...])
    e0, e1, g0, g1 = _route(logits_t, ts)
    eidx_ref[0:1, pl.ds(r0, ts)] = e0
    eidx_ref[1:2, pl.ds(r0, ts)] = e1
    gate_ref[0:1, pl.ds(r0, ts)] = g0
    gate_ref[1:2, pl.ds(r0, ts)] = g1

    eid = lax.broadcasted_iota(jnp.int32, (N_EXPERTS, ts), 0)
    hit0 = eid == e0
    hit1 = eid == e1
    onehot = hit0.astype(F32) + hit1.astype(F32)
    t_r = lax.broadcasted_iota(jnp.int32, (ts, ts), 0)
    t_c = lax.broadcasted_iota(jnp.int32, (ts, ts), 1)
    upper = (t_r <= t_c).astype(BF16)
    csum = _dot(onehot.astype(BF16), upper)
    seen = jnp.where(first_tile, 0.0, cnt_scr[...])
    before = csum - onehot + seen[:, 0:1]
    rank_ref[0:1, pl.ds(r0, ts)] = jnp.sum(jnp.where(hit0, before, 0.0), axis=0, keepdims=True).astype(jnp.int32)
    rank_ref[1:2, pl.ds(r0, ts)] = jnp.sum(jnp.where(hit1, before, 0.0), axis=0, keepdims=True).astype(jnp.int32)
    seen = seen + csum[:, ts - 1:ts]
    cnt_scr[...] = seen
    cnt_ref[...] = seen.astype(jnp.int32)


def _mixer_kernel(x_ref, xn_ref, w_ref, b_ref, cw_ref, mhg_ref, wo_ref, lng_ref, lnb_ref,
                  rwh_ref, rwl_ref, rb_ref,
                  xo_ref, eidx_ref, gate_ref, rank_ref, cnt_ref,
                  pa_ref, pb_ref, y_ref, state_ref, m_ref, ucarry_ref, cnt_scr, *, ts, seq):
    step = pl.program_id(0)
    refs = (cw_ref, mhg_ref, wo_ref, lng_ref, lnb_ref, rwh_ref, rwl_ref, rb_ref,
            xo_ref, eidx_ref, gate_ref, rank_ref, cnt_ref,
            y_ref, state_ref, m_ref, ucarry_ref, cnt_scr)

    @pl.when(step == 0)
    def _():
        _in_proj(x_ref[0:ts, :], w_ref, b_ref, pa_ref)

    tile_a = 2 * step
    _in_proj(x_ref[ts:2 * ts, :], w_ref, b_ref, pb_ref)
    _mix_tile(pa_ref, x_ref[0:ts, :], 0, (tile_a * ts) % seq == 0, tile_a == 0, refs, ts)
    _in_proj(xn_ref[...], w_ref, b_ref, pa_ref)
    _mix_tile(pb_ref, x_ref[ts:2 * ts, :], ts, ((tile_a + 1) * ts) % seq == 0, False, refs, ts)


def _const_spec(shape):
    nd = len(shape)
    return pl.BlockSpec(shape, lambda *_: (0,) * nd, pipeline_mode=pl.Buffered(1))


def _mixer_call(x_flat, seq, w_all, b_all, conv_w, mh_g, w_out, ln_g, ln_b, rw_hi, rw_lo, rb):
    n, d = x_flat.shape
    ts = min(MIX_ROWS, seq // 2)
    n_steps = n // (2 * ts)
    tok_spec = pl.BlockSpec((TOP_K, 2 * ts), lambda i: (0, i))
    return pl.pallas_call(
        functools.partial(_mixer_kernel, ts=ts, seq=seq),
        grid=(n_steps,),
        in_specs=[
            pl.BlockSpec((2 * ts, d), lambda i: (i, 0)),
            pl.BlockSpec((ts, d), lambda i: (2 * jnp.minimum(i + 1, n_steps - 1), 0)),
            _const_spec(w_all.shape), _const_spec(b_all.shape), _const_spec(conv_w.shape),
            _const_spec(mh_g.shape), _const_spec(w_out.shape), _const_spec(ln_g.shape),
            _const_spec(ln_b.shape), _const_spec(rw_hi.shape), _const_spec(rw_lo.shape),
            _const_spec(rb.shape),
        ],
        out_specs=[
            pl.BlockSpec((2 * ts, d), lambda i: (i, 0)),
            tok_spec, tok_spec, tok_spec,
            pl.BlockSpec((N_EXPERTS, LANES), lambda i: (0, 0)),
        ],
        out_shape=[
            jax.ShapeDtypeStruct((n, d), F32),
            jax.ShapeDtypeStruct((TOP_K, n), jnp.int32),
            jax.ShapeDtypeStruct((TOP_K, n), F32),
            jax.ShapeDtypeStruct((TOP_K, n), jnp.int32),
            jax.ShapeDtypeStruct((N_EXPERTS, LANES), jnp.int32),
        ],
        scratch_shapes=[
            pltpu.VMEM((ts, W_ALL), F32),
            pltpu.VMEM((ts, W_ALL), F32),
            pltpu.VMEM((ts, D_MODEL), BF16),
            pltpu.VMEM((N_HEADS, HEAD_DIM, 2 * HEAD_DIM), F32),
            pltpu.VMEM((8, LANES), F32),
            pltpu.VMEM((8, D_CONV), F32),
            pltpu.VMEM((N_EXPERTS, LANES), F32),
        ],
        compiler_params=pltpu.CompilerParams(
            dimension_semantics=("arbitrary",), vmem_limit_bytes=VMEM_LIMIT),
        name="mixer",
    )(x_flat, x_flat, w_all, b_all, conv_w, mh_g, w_out, ln_g, ln_b, rw_hi, rw_lo, rb)


def _dispatch_kernel(pe_ref, dest_ref, x_ref, buf_ref, zero_ref, sem, zsem, *, rows):
    @pl.when(pl.program_id(0) == 0)
    def _():
        zero_ref[...] = jnp.zeros_like(zero_ref)

        def fill(start):
            start = pl.multiple_of(start, MOE_BLOCK)
            return pltpu.make_async_copy(zero_ref, buf_ref.at[pl.ds(start, MOE_BLOCK)], zsem)

        used = pe_ref[N_EXPERTS - 1]
        fills = [(pe_ref[e] - MOE_BLOCK, pe_ref[e] > (pe_ref[e - 1] if e else 0))
                 for e in range(N_EXPERTS)]
        fills += [(used + t * MOE_BLOCK, used + t * MOE_BLOCK < buf_ref.shape[0])
                  for t in range(N_EXPERTS)]
        for start, pred in fills:
            pl.when(pred)(lambda start=start: fill(start).start())
        for start, pred in fills:
            pl.when(pred)(lambda start=start: fill(start).wait())

    def issue(i, carry):
        for k in range(TOP_K):
            pltpu.make_async_copy(
                x_ref.at[pl.ds(i, 1)], buf_ref.at[pl.ds(dest_ref[k, i], 1)], sem.at[k]).start()
        return carry

    lax.fori_loop(0, rows, issue, 0, unroll=8)
    for k in range(TOP_K):
        pltpu.make_async_copy(x_ref, buf_ref.at[pl.ds(0, rows)], sem.at[k]).wait()


def _dispatch_call(x_flat, dest, pad_end, n_slots):
    n, d = x_flat.shape
    rows = min(ROW_TILE, n)
    return pl.pallas_call(
        functools.partial(_dispatch_kernel, rows=rows),
        grid_spec=pltpu.PrefetchScalarGridSpec(
            num_scalar_prefetch=1,
            grid=(n // rows,),
            in_specs=[
                pl.BlockSpec((TOP_K, rows), lambda i, pe: (0, i), memory_space=pltpu.SMEM),
                pl.BlockSpec((rows, d), lambda i, pe: (i, 0)),
            ],
            out_specs=pl.BlockSpec(memory_space=pl.ANY),
            scratch_shapes=[
                pltpu.VMEM((MOE_BLOCK, d), x_flat.dtype),
                pltpu.SemaphoreType.DMA((TOP_K,)),
                pltpu.SemaphoreType.DMA(()),
            ],
        ),
        out_shape=jax.ShapeDtypeStruct((n_slots, d), x_flat.dtype),
        compiler_params=pltpu.CompilerParams(
            dimension_semantics=("arbitrary",), has_side_effects=True),
        name="dispatch",
    )(pad_end, dest, x_flat)


def _expert_kernel(be_ref, nv_ref, buf_ref, wgu_ref, wd_ref, y_ref):
    used = pl.program_id(0) < nv_ref[0]

    @pl.when(used)
    def _():
        h = _dot(buf_ref[...].astype(BF16), wgu_ref[0])
        hdn = jax.nn.silu(h[:, :D_FF]) * h[:, D_FF:]
        y_ref[...] = _dot(hdn.astype(BF16), wd_ref[0])

    @pl.when(jnp.logical_not(used))
    def _():
        y_ref[...] = jnp.zeros_like(y_ref)


def _expert_call(block_expert, n_valid, buf, wgu, wd):
    n_slots, d = buf.shape
    n_blocks = n_slots // MOE_BLOCK

    def row_map(i, be, nv):
        return (jnp.minimum(i, nv[0] - 1), 0)

    def w_map(i, be, nv):
        return (be[jnp.minimum(i, nv[0] - 1)], 0, 0)

    return pl.pallas_call(
        _expert_kernel,
        grid_spec=pltpu.PrefetchScalarGridSpec(
            num_scalar_prefetch=2,
            grid=(n_blocks,),
            in_specs=[
                pl.BlockSpec((MOE_BLOCK, d), row_map),
                pl.BlockSpec((1, d, 2 * D_FF), w_map),
                pl.BlockSpec((1, D_FF, d), w_map),
            ],
            out_specs=pl.BlockSpec((MOE_BLOCK, d), lambda i, be, nv: (i, 0)),
        ),
        out_shape=jax.ShapeDtypeStruct((n_slots, d), F32),
        compiler_params=pltpu.CompilerParams(
            dimension_semantics=("arbitrary",), vmem_limit_bytes=VMEM_LIMIT),
        name="experts",
    )(block_expert, n_valid, buf, wgu, wd)


def _combine_kernel(dest_ref, dnext_ref, x_ref, gate_ref, lng_ref, lnb_ref, y_hbm, o_ref, ybuf, sem,
                    *, rows):
    step = pl.program_id(0)
    slot = step % 2

    def gather(idx_ref, s):
        def issue(i, carry):
            for k in range(TOP_K):
                pltpu.make_async_copy(
                    y_hbm.at[pl.ds(idx_ref[k, i], 1)], ybuf.at[s, k, pl.ds(i, 1)], sem.at[s, k]).start()
            return carry
        lax.fori_loop(0, rows, issue, 0, unroll=8)

    @pl.when(step == 0)
    def _():
        gather(dest_ref, slot)

    @pl.when(step + 1 < pl.num_programs(0))
    def _():
        gather(dnext_ref, 1 - slot)

    for k in range(TOP_K):
        pltpu.make_async_copy(y_hbm.at[pl.ds(0, rows)], ybuf.at[slot, k], sem.at[slot, k]).wait()
    ffn = ybuf[slot, 0] * gate_ref[:, 0:1] + ybuf[slot, 1] * gate_ref[:, 1:2]
    o_ref[...] = _layer_norm(ALPHA * x_ref[...] + ffn, lng_ref[...], lnb_ref[...])


def _combine_call(x_flat, dest, gates_t, y_buf, ln_g, ln_b):
    n, d = x_flat.shape
    rows = min(ROW_TILE, n)
    last = n // rows - 1
    return pl.pallas_call(
        functools.partial(_combine_kernel, rows=rows),
        grid=(n // rows,),
        in_specs=[
            pl.BlockSpec((TOP_K, rows), lambda i: (0, i), memory_space=pltpu.SMEM),
            pl.BlockSpec((TOP_K, rows), lambda i: (0, jnp.minimum(i + 1, last)), memory_space=pltpu.SMEM),
            pl.BlockSpec((rows, d), lambda i: (i, 0)),
            pl.BlockSpec((rows, TOP_K), lambda i: (i, 0)),
            _const_spec(ln_g.shape), _const_spec(ln_b.shape),
            pl.BlockSpec(memory_space=pl.ANY),
        ],
        out_specs=pl.BlockSpec((rows, d), lambda i: (i, 0)),
        out_shape=jax.ShapeDtypeStruct((n, d), F32),
        scratch_shapes=[pltpu.VMEM((2, TOP_K, rows, d), F32), pltpu.SemaphoreType.DMA((2, TOP_K))],
        compiler_params=pltpu.CompilerParams(dimension_semantics=("arbitrary",)),
        name="combine",
    )(dest, dest, x_flat, gates_t, ln_g, ln_b, y_buf)


def _slot_plan(counts, eidx, rank, n_blocks):
    padded = (counts + MOE_BLOCK - 1) // MOE_BLOCK * MOE_BLOCK
    pad_end = jnp.cumsum(padded)
    pad_start = pad_end - padded
    dest = rank
    for e in range(N_EXPERTS):
        dest = dest + jnp.where(eidx == e, pad_start[e], 0)
    block_expert = jnp.minimum(
        jnp.searchsorted(pad_end, jnp.arange(n_blocks, dtype=jnp.int32) * MOE_BLOCK, side="right"),
        N_EXPERTS - 1).astype(jnp.int32)
    n_valid = (pad_end[-1:] // MOE_BLOCK).astype(jnp.int32)
    return dest.astype(jnp.int32), pad_end.astype(jnp.int32), block_expert, n_valid


def kernel(x, w_in, b_in, conv_w, mh_norm_g, w_out, ln_mix_g, ln_mix_b, router_w, router_b,
           w_gate, w_up, w_down, ln_moe_g, ln_moe_b):
    bsz, seq, d = x.shape
    n = bsz * seq
    n_slots = n * TOP_K + N_EXPERTS * MOE_BLOCK
    n_blocks = n_slots // MOE_BLOCK

    gate_pad = jnp.zeros(w_in.shape[:2] + (LANES - N_HEADS,), w_in.dtype)
    w_all = jnp.concatenate(
        [w_in[..., OFF_Q:OFF_I], w_in[..., OFF_B:P_IN], w_in[..., OFF_I:OFF_F], gate_pad,
         w_in[..., OFF_F:OFF_B], gate_pad], axis=-1).astype(BF16)
    b_all = jnp.concatenate(
        [b_in[..., OFF_Q:OFF_I], b_in[..., OFF_B:P_IN], b_in[..., OFF_I:OFF_F], gate_pad[:, 0, :],
         b_in[..., OFF_F:OFF_B], gate_pad[:, 0, :]], axis=-1).astype(F32)[:, None, :]
    w_out_b = w_out.astype(BF16)
    rw_t = router_w.T.astype(F32)
    rw_hi, rw_lo = _split_bf16(rw_t)
    rb = router_b.astype(F32)[:, None]
    wgu = jnp.concatenate([w_gate, w_up], axis=-1).astype(BF16)
    wd = w_down.astype(BF16)

    xf = x.reshape(n, d)
    for l in range(DEPTH):
        x1, eidx, gates, rank, cnt = _mixer_call(
            xf, seq, w_all[l], b_all[l], conv_w[l], mh_norm_g[l][None, :], w_out_b[l],
            ln_mix_g[l][None, :], ln_mix_b[l][None, :], rw_hi, rw_lo, rb)
        dest, pad_end, block_expert, n_valid = _slot_plan(cnt[:, 0], eidx, rank, n_blocks)
        buf = _dispatch_call(x1, dest, pad_end, n_slots)
        y_buf = _expert_call(block_expert, n_valid, buf, wgu[l], wd[l])
        xf = _combine_call(x1, dest, gates.T, y_buf, ln_moe_g[l][None, :], ln_moe_b[l][None, :])
    return xf.reshape(bsz, seq, d)
```

```python
import functools

import jax
import jax.numpy as jnp
from jax import lax
from jax.experimental import pallas as pl
from jax.experimental.pallas import tpu as pltpu
from jax.experimental.pallas import tpu_sc as plsc

D_MODEL = 1024
DEPTH = 4
D_MLSTM = D_MODEL // 2
D_CONV = D_MODEL - D_MLSTM
N_HEADS = 4
HEAD_DIM = D_MLSTM // N_HEADS
CONV_WIDTH = 3
CHUNK = 128
N_EXPERTS = 16
N_GROUPS = 4
EXPERTS_PER_GROUP = N_EXPERTS // N_GROUPS
TOP_K = 2
D_FF = D_MODEL // 2
LN_EPS = 1e-5
ALPHA = (2 * DEPTH) ** 0.25

OFF_Q = 0
OFF_K = OFF_Q + D_MLSTM
OFF_V = OFF_K + D_MLSTM
OFF_O = OFF_V + D_MLSTM
OFF_I = OFF_O + D_MLSTM
OFF_F = OFF_I + N_HEADS
OFF_B = OFF_F + N_HEADS
OFF_C = OFF_B + D_CONV
OFF_X = OFF_C + D_CONV
P_IN = OFF_X + D_CONV

LANES = 128
W_M = 4 * D_MLSTM
W_C = 3 * D_CONV
GATE_I = W_M + W_C
GATE_F = GATE_I + LANES
W_ALL = GATE_F + LANES

MIX_ROWS = 256
MOE_BLOCK = 512
ROW_TILE = 256
SC_WINDOW = 32
VMEM_LIMIT = 56 * 1024 * 1024

F32 = jnp.float32
BF16 = jnp.bfloat16


def _dot(a, b):
    return jnp.dot(a, b, preferred_element_type=F32)


def _dot_nt(a, b):
    return lax.dot_general(a, b, (((1,), (1,)), ((), ())), preferred_element_type=F32)


def _split_bf16(x):
    hi = x.astype(BF16)
    lo = (x - hi.astype(F32)).astype(BF16)
    return hi, lo


def _layer_norm(z, g, b):
    mu = jnp.mean(z, axis=-1, keepdims=True)
    d = z - mu
    var = jnp.mean(d * d, axis=-1, keepdims=True)
    return d * lax.rsqrt(var + LN_EPS) * g + b


def _rows(x, c):
    return x[c * CHUNK:(c + 1) * CHUNK]


def _stack(parts):
    return parts[0] if len(parts) == 1 else jnp.concatenate(parts, axis=0)


def _mlstm_tile(p_ref, y_ref, state_ref, m_ref, mhg_ref, fresh, ts):
    L = CHUNK
    nch = ts // L
    row_i = lax.broadcasted_iota(jnp.int32, (L, L), 0)
    col_i = lax.broadcasted_iota(jnp.int32, (L, L), 1)
    tri = (col_i <= row_i).astype(BF16)
    causal = _stack([col_i <= row_i] * nch)
    one_col = (lax.broadcasted_iota(jnp.int32, (ts, LANES), 1) == 0).astype(BF16)

    ig = p_ref[:, pl.ds(GATE_I, LANES)]
    fg = p_ref[:, pl.ds(GATE_F, LANES)]
    logf = jnp.minimum(fg, 0.0) - jnp.log1p(jnp.exp(-jnp.abs(fg)))
    lf_hi, lf_lo = _split_bf16(logf)
    a = _stack([_dot(tri, _rows(lf_hi, c)) + _dot(tri, _rows(lf_lo, c)) for c in range(nch)])
    g = [a[(c + 1) * L - 1:(c + 1) * L, :] for c in range(nch)]
    w_end = _stack([g[c] - _rows(a, c) + _rows(ig, c) for c in range(nch)])
    m_loc = [jnp.max(_rows(w_end, c), axis=0, keepdims=True) for c in range(nch)]
    m = [jnp.where(fresh, 0.0, m_ref[0:1, :])]
    for c in range(nch):
        m.append(jnp.maximum(g[c] + m[c], m_loc[c]))
    m_ref[0:1, :] = m[nch]
    log_inter = a + _stack([jnp.broadcast_to(m[c], (L, LANES)) for c in range(nch)])
    e_end = jnp.exp(w_end - _stack([jnp.broadcast_to(m_loc[c], (L, LANES)) for c in range(nch)]))
    keep = [jnp.exp(g[c] + m[c] - m[c + 1]) for c in range(nch)]
    take = [jnp.exp(m_loc[c] - m[c + 1]) for c in range(nch)]
    b_t = [(_rows(ig, c) - _rows(a, c)).T for c in range(nch)]

    for h in range(N_HEADS):
        c0 = h * HEAD_DIM
        q = p_ref[:, pl.ds(OFF_Q + c0, HEAD_DIM)]
        k = p_ref[:, pl.ds(OFF_K + c0, HEAD_DIM)] * (HEAD_DIM ** -0.5)
        v = p_ref[:, pl.ds(OFF_V + c0, HEAD_DIM)]
        o = p_ref[:, pl.ds(OFF_O + c0, HEAD_DIM)]

        b_rows = _stack([jnp.broadcast_to(b_t[c][h:h + 1, :], (L, L)) for c in range(nch)])
        log_d = jnp.where(causal, a[:, h:h + 1] + b_rows, -jnp.inf)
        m_out = jnp.maximum(log_inter[:, h:h + 1], jnp.max(log_d, axis=-1, keepdims=True))
        dmat = jnp.exp(log_d - m_out)
        qb = q.astype(BF16)
        kb = k.astype(BF16)
        v_aug = jnp.concatenate([v.astype(BF16), one_col], axis=1)
        s_qk = (_stack([_dot_nt(_rows(qb, c), _rows(kb, c)) for c in range(nch)]) * dmat).astype(BF16)
        intra = _stack([_dot(_rows(s_qk, c), _rows(v_aug, c)) for c in range(nch)])
        k_e = k * e_end[:, h:h + 1]
        s_loc = [_dot(_rows(k_e, c).T.astype(BF16), _rows(v_aug, c)) for c in range(nch)]
        state = [jnp.where(fresh, 0.0, state_ref[h])]
        for c in range(nch):
            state.append(keep[c][:, h:h + 1] * state[c] + take[c][:, h:h + 1] * s_loc[c])
        state_ref[h] = state[nch]
        inter = _stack([_dot(_rows(qb, c), state[c].astype(BF16)) for c in range(nch)])
        tot = intra + jnp.exp(log_inter[:, h:h + 1] - m_out) * inter
        num = tot[:, :HEAD_DIM]
        den = tot[:, HEAD_DIM:HEAD_DIM + 1]
        hh = num / jnp.maximum(jnp.abs(den), jnp.exp(-m_out))

        mu = jnp.mean(hh, axis=-1, keepdims=True)
        dh = hh - mu
        var = jnp.mean(dh * dh, axis=-1, keepdims=True)
        hn = dh * lax.rsqrt(var + LN_EPS) * mhg_ref[:, pl.ds(c0, HEAD_DIM)]
        y_ref[:, pl.ds(c0, HEAD_DIM)] = (jax.nn.sigmoid(o) * hn).astype(y_ref.dtype)


def _route(logits_t, ts):
    lg = [logits_t[e:e + 1, :] for e in range(N_EXPERTS)]
    mx = functools.reduce(jnp.maximum, lg)
    ex = [jnp.exp(v - mx) for v in lg]
    tot = functools.reduce(lambda a, b: a + b, ex)
    p = [v / tot for v in ex]
    scores = []
    for g in range(N_GROUPS):
        p0, p1, p2, p3 = p[g * EXPERTS_PER_GROUP:(g + 1) * EXPERTS_PER_GROUP]
        hi01, lo01 = jnp.maximum(p0, p1), jnp.minimum(p0, p1)
        hi23, lo23 = jnp.maximum(p2, p3), jnp.minimum(p2, p3)
        top1 = jnp.maximum(hi01, hi23)
        top2 = jnp.maximum(jnp.minimum(hi01, hi23), jnp.maximum(lo01, lo23))
        scores.append(top1 + top2)
    grp = jnp.zeros((1, ts), jnp.int32)
    best = scores[0]
    for g in range(1, N_GROUPS):
        upd = scores[g] > best
        best = jnp.where(upd, scores[g], best)
        grp = jnp.where(upd, g, grp)
    sel = []
    for j in range(EXPERTS_PER_GROUP):
        v = p[j]
        for g in range(1, N_GROUPS):
            v = jnp.where(grp == g, p[g * EXPERTS_PER_GROUP + j], v)
        sel.append(v)
    i1 = jnp.zeros((1, ts), jnp.int32)
    v1 = sel[0]
    for j in range(1, EXPERTS_PER_GROUP):
        upd = sel[j] > v1
        v1 = jnp.where(upd, sel[j], v1)
        i1 = jnp.where(upd, j, i1)
    i2 = jnp.zeros((1, ts), jnp.int32)
    v2 = jnp.full((1, ts), -jnp.inf, F32)
    for j in range(EXPERTS_PER_GROUP):
        cand = jnp.where(i1 == j, -jnp.inf, sel[j])
        upd = cand > v2
        v2 = jnp.where(upd, cand, v2)
        i2 = jnp.where(upd, j, i2)
    e0 = grp * EXPERTS_PER_GROUP + i1
    e1 = grp * EXPERTS_PER_GROUP + i2
    s12 = v1 + v2
    return e0, e1, v1 / s12, v2 / s12


def _in_proj(x, w_ref, b_ref, p_ref):
    p_ref[...] = _dot(x.astype(BF16), w_ref[...]) + b_ref[...]


def _mix_tile(p_ref, x, r0, fresh, first_tile, refs, ts):
    (cw_ref, mhg_ref, wo_ref, lng_ref, lnb_ref, rwh_ref, rwl_ref, rb_ref,
     xo_ref, eidx_ref, gate_ref, rank_ref, cnt_ref,
     y_ref, state_ref, m_ref, ucarry_ref, cnt_scr) = refs
    _mlstm_tile(p_ref, y_ref, state_ref, m_ref, mhg_ref, fresh, ts)

    gate_b = p_ref[:, pl.ds(W_M, D_CONV)]
    u = p_ref[:, pl.ds(W_M + D_CONV, D_CONV)] * p_ref[:, pl.ds(W_M + 2 * D_CONV, D_CONV)]
    rid = lax.broadcasted_iota(jnp.int32, (ts, D_CONV), 0)
    prev1 = jnp.where(fresh, 0.0, ucarry_ref[7:8, :])
    prev2 = jnp.where(fresh, 0.0, ucarry_ref[6:7, :])
    u1 = jnp.where(rid == 0, prev1, pltpu.roll(u, 1, 0))
    u2 = jnp.where(rid == 0, prev2, jnp.where(rid == 1, prev1, pltpu.roll(u, 2, 0)))
    conv = cw_ref[0:1, :] * u2 + cw_ref[1:2, :] * u1 + cw_ref[2:3, :] * u
    y_ref[:, pl.ds(D_MLSTM, D_CONV)] = (gate_b * conv).astype(y_ref.dtype)
    ucarry_ref[...] = u[ts - 8:ts, :]

    mix = _dot(y_ref[...], wo_ref[...])
    x1 = _layer_norm(ALPHA * x + mix, lng_ref[...], lnb_ref[...])
    xo_ref[pl.ds(r0, ts), :] = x1

    x_hi, x_lo = _split_bf16(x1)
    logits_t = (_dot_nt(rwh_ref[...], x_hi) + _dot_nt(rwl_ref[...], x_hi)
                + _dot_nt(rwh_ref[...], x_lo) + rb_ref[...])
    e0, e1, g0, g1 = _route(logits_t, ts)
    eidx_ref[0:1, pl.ds(r0, ts)] = e0
    eidx_ref[1:2, pl.ds(r0, ts)] = e1
    gate_ref[0:1, pl.ds(r0, ts)] = g0
    gate_ref[1:2, pl.ds(r0, ts)] = g1

    eid = lax.broadcasted_iota(jnp.int32, (N_EXPERTS, ts), 0)
    hit0 = eid == e0
    hit1 = eid == e1
    onehot = hit0.astype(F32) + hit1.astype(F32)
    t_r = lax.broadcasted_iota(jnp.int32, (ts, ts), 0)
    t_c = lax.broadcasted_iota(jnp.int32, (ts, ts), 1)
    upper = (t_r <= t_c).astype(BF16)
    csum = _dot(onehot.astype(BF16), upper)
    seen = jnp.where(first_tile, 0.0, cnt_scr[...])
    before = csum - onehot + seen[:, 0:1]
    rank_ref[0:1, pl.ds(r0, ts)] = jnp.sum(jnp.where(hit0, before, 0.0), axis=0, keepdims=True).astype(jnp.int32)
    rank_ref[1:2, pl.ds(r0, ts)] = jnp.sum(jnp.where(hit1, before, 0.0), axis=0, keepdims=True).astype(jnp.int32)
    seen = seen + csum[:, ts - 1:ts]
    cnt_scr[...] = seen
    cnt_ref[...] = seen.astype(jnp.int32)


def _mixer_kernel(x_ref, xn_ref, w_ref, b_ref, cw_ref, mhg_ref, wo_ref, lng_ref, lnb_ref,
                  rwh_ref, rwl_ref, rb_ref,
                  xo_ref, eidx_ref, gate_ref, rank_ref, cnt_ref,
                  pa_ref, pb_ref, y_ref, state_ref, m_ref, ucarry_ref, cnt_scr, *, ts, seq):
    step = pl.program_id(0)
    refs = (cw_ref, mhg_ref, wo_ref, lng_ref, lnb_ref, rwh_ref, rwl_ref, rb_ref,
            xo_ref, eidx_ref, gate_ref, rank_ref, cnt_ref,
            y_ref, state_ref, m_ref, ucarry_ref, cnt_scr)

    @pl.when(step == 0)
    def _():
        _in_proj(x_ref[0:ts, :], w_ref, b_ref, pa_ref)

    tile_a = 2 * step
    _in_proj(x_ref[ts:2 * ts, :], w_ref, b_ref, pb_ref)
    _mix_tile(pa_ref, x_ref[0:ts, :], 0, (tile_a * ts) % seq == 0, tile_a == 0, refs, ts)
    _in_proj(xn_ref[...], w_ref, b_ref, pa_ref)
    _mix_tile(pb_ref, x_ref[ts:2 * ts, :], ts, ((tile_a + 1) * ts) % seq == 0, False, refs, ts)


def _const_spec(shape):
    nd = len(shape)
    return pl.BlockSpec(shape, lambda *_: (0,) * nd, pipeline_mode=pl.Buffered(1))


def _mixer_call(x_flat, seq, w_all, b_all, conv_w, mh_g, w_out, ln_g, ln_b, rw_hi, rw_lo, rb):
    n, d = x_flat.shape
    ts = min(MIX_ROWS, seq // 2)
    n_steps = n // (2 * ts)
    tok_spec = pl.BlockSpec((TOP_K, 2 * ts), lambda i: (0, i))
    return pl.pallas_call(
        functools.partial(_mixer_kernel, ts=ts, seq=seq),
        grid=(n_steps,),
        in_specs=[
            pl.BlockSpec((2 * ts, d), lambda i: (i, 0)),
            pl.BlockSpec((ts, d), lambda i: (2 * jnp.minimum(i + 1, n_steps - 1), 0)),
            _const_spec(w_all.shape), _const_spec(b_all.shape), _const_spec(conv_w.shape),
            _const_spec(mh_g.shape), _const_spec(w_out.shape), _const_spec(ln_g.shape),
            _const_spec(ln_b.shape), _const_spec(rw_hi.shape), _const_spec(rw_lo.shape),
            _const_spec(rb.shape),
        ],
        out_specs=[
            pl.BlockSpec((2 * ts, d), lambda i: (i, 0)),
            tok_spec, tok_spec, tok_spec,
            pl.BlockSpec((N_EXPERTS, LANES), lambda i: (0, 0)),
        ],
        out_shape=[
            jax.ShapeDtypeStruct((n, d), F32),
            jax.ShapeDtypeStruct((TOP_K, n), jnp.int32),
            jax.ShapeDtypeStruct((TOP_K, n), F32),
            jax.ShapeDtypeStruct((TOP_K, n), jnp.int32),
            jax.ShapeDtypeStruct((N_EXPERTS, LANES), jnp.int32),
        ],
        scratch_shapes=[
            pltpu.VMEM((ts, W_ALL), F32),
            pltpu.VMEM((ts, W_ALL), F32),
            pltpu.VMEM((ts, D_MODEL), BF16),
            pltpu.VMEM((N_HEADS, HEAD_DIM, 2 * HEAD_DIM), F32),
            pltpu.VMEM((8, LANES), F32),
            pltpu.VMEM((8, D_CONV), F32),
            pltpu.VMEM((N_EXPERTS, LANES), F32),
        ],
        compiler_params=pltpu.CompilerParams(
            dimension_semantics=("arbitrary",), vmem_limit_bytes=VMEM_LIMIT),
        name="mixer",
    )(x_flat, x_flat, w_all, b_all, conv_w, mh_g, w_out, ln_g, ln_b, rw_hi, rw_lo, rb)


def _sc_mesh():
    return plsc.VectorSubcoreMesh(core_axis_name="core", subcore_axis_name="subcore")


def _sc_scatter_pair(x_flat, dest, n_slots):
    n, d = x_flat.shape
    windows = dest.reshape(TOP_K, n // SC_WINDOW, SC_WINDOW).transpose(1, 0, 2)

    @functools.partial(
        pl.kernel, out_type=jax.ShapeDtypeStruct((n_slots, d), x_flat.dtype), mesh=_sc_mesh(),
        scratch_types=[])
    def scatter_kernel(x_hbm, i_hbm, o_hbm):
        def body(x_vmem, i_vmem):
            for k in range(TOP_K):
                pltpu.sync_copy(x_vmem, o_hbm.at[i_vmem.at[0, k]])

        pltpu.emit_pipeline(
            body,
            grid=(n // SC_WINDOW,),
            in_specs=[pl.BlockSpec((SC_WINDOW, d), lambda i: (i, 0)),
                      pl.BlockSpec((1, TOP_K, SC_WINDOW), lambda i: (i, 0, 0))],
            out_specs=[],
            core_axis_name=("core", "subcore"),
            dimension_semantics=(pltpu.PARALLEL,),
        )(x_hbm, i_hbm)

    return scatter_kernel(x_flat, windows)


def _expert_kernel(be_ref, nv_ref, br_ref, buf_ref, wgu_ref, wd_ref, y_ref):
    used = pl.program_id(0) < nv_ref[0]

    @pl.when(used)
    def _():
        rid = lax.broadcasted_iota(jnp.int32, (MOE_BLOCK, 1), 0)
        xb = jnp.where(rid < br_ref[pl.program_id(0)], buf_ref[...], 0.0).astype(BF16)
        h = _dot(xb, wgu_ref[0])
        hdn = jax.nn.silu(h[:, :D_FF]) * h[:, D_FF:]
        y_ref[...] = _dot(hdn.astype(BF16), wd_ref[0])

    @pl.when(jnp.logical_not(used))
    def _():
        y_ref[...] = jnp.zeros_like(y_ref)


def _expert_call(block_expert, n_valid, block_rows, buf, wgu, wd):
    n_slots, d = buf.shape
    n_blocks = n_slots // MOE_BLOCK

    def row_map(i, be, nv, br):
        return (jnp.minimum(i, nv[0] - 1), 0)

    def w_map(i, be, nv, br):
        return (be[jnp.minimum(i, nv[0] - 1)], 0, 0)

    return pl.pallas_call(
        _expert_kernel,
        grid_spec=pltpu.PrefetchScalarGridSpec(
            num_scalar_prefetch=3,
            grid=(n_blocks,),
            in_specs=[
                pl.BlockSpec((MOE_BLOCK, d), row_map),
                pl.BlockSpec((1, d, 2 * D_FF), w_map),
                pl.BlockSpec((1, D_FF, d), w_map),
            ],
            out_specs=pl.BlockSpec((MOE_BLOCK, d), lambda i, be, nv, br: (i, 0)),
        ),
        out_shape=jax.ShapeDtypeStruct((n_slots, d), F32),
        compiler_params=pltpu.CompilerParams(
            dimension_semantics=("arbitrary",), vmem_limit_bytes=VMEM_LIMIT),
        name="experts",
    )(block_expert, n_valid, block_rows, buf, wgu, wd)


def _sc_gather(table, idx):
    m = idx.shape[0]
    d = table.shape[1]

    @functools.partial(
        pl.kernel, out_type=jax.ShapeDtypeStruct((m, d), table.dtype), mesh=_sc_mesh(),
        scratch_types=[])
    def gather_kernel(t_hbm, i_hbm, o_hbm):
        def body(i_vmem, o_vmem):
            pltpu.sync_copy(t_hbm.at[i_vmem.at[0, 0]], o_vmem)

        pltpu.emit_pipeline(
            body,
            grid=(m // SC_WINDOW,),
            in_specs=[pl.BlockSpec((1, 1, SC_WINDOW), lambda i: (i, 0, 0))],
            out_specs=[pl.BlockSpec((SC_WINDOW, d), lambda i: (i, 0))],
            core_axis_name=("core", "subcore"),
            dimension_semantics=(pltpu.PARALLEL,),
        )(i_hbm, o_hbm)

    return gather_kernel(table, idx.reshape(m // SC_WINDOW, 1, SC_WINDOW))


def _combine_kernel(x_ref, y0_ref, y1_ref, gate_ref, lng_ref, lnb_ref, o_ref):
    ffn = y0_ref[...] * gate_ref[:, 0:1] + y1_ref[...] * gate_ref[:, 1:2]
    o_ref[...] = _layer_norm(ALPHA * x_ref[...] + ffn, lng_ref[...], lnb_ref[...])


def _combine_call(x_flat, dest, gates_t, y_buf, ln_g, ln_b):
    n, d = x_flat.shape
    rows = min(2 * ROW_TILE, n)
    nb = n // rows
    yg = _sc_gather(y_buf, dest.reshape(TOP_K * n))
    return pl.pallas_call(
        _combine_kernel,
        grid=(nb,),
        in_specs=[
            pl.BlockSpec((rows, d), lambda i: (i, 0)),
            pl.BlockSpec((rows, d), lambda i: (i, 0)),
            pl.BlockSpec((rows, d), lambda i: (i + nb, 0)),
            pl.BlockSpec((rows, TOP_K), lambda i: (i, 0)),
            _const_spec(ln_g.shape), _const_spec(ln_b.shape),
        ],
        out_specs=pl.BlockSpec((rows, d), lambda i: (i, 0)),
        out_shape=jax.ShapeDtypeStruct((n, d), F32),
        compiler_params=pltpu.CompilerParams(
            dimension_semantics=("arbitrary",), vmem_limit_bytes=VMEM_LIMIT),
        name="combine",
    )(x_flat, yg, yg, gates_t, ln_g, ln_b)


def _slot_plan(counts, eidx, rank, n_blocks):
    padded = (counts + MOE_BLOCK - 1) // MOE_BLOCK * MOE_BLOCK
    pad_end = jnp.cumsum(padded)
    pad_start = pad_end - padded
    dest = rank
    for e in range(N_EXPERTS):
        dest = dest + jnp.where(eidx == e, pad_start[e], 0)
    block_expert = jnp.minimum(
        jnp.searchsorted(pad_end, jnp.arange(n_blocks, dtype=jnp.int32) * MOE_BLOCK, side="right"),
        N_EXPERTS - 1).astype(jnp.int32)
    n_valid = (pad_end[-1:] // MOE_BLOCK).astype(jnp.int32)
    block_row0 = jnp.arange(n_blocks, dtype=jnp.int32) * MOE_BLOCK
    block_rows = jnp.clip(
        (pad_start + counts)[block_expert] - block_row0, 0, MOE_BLOCK).astype(jnp.int32)
    return dest.astype(jnp.int32), block_expert, n_valid, block_rows


def kernel(x, w_in, b_in, conv_w, mh_norm_g, w_out, ln_mix_g, ln_mix_b, router_w, router_b,
           w_gate, w_up, w_down, ln_moe_g, ln_moe_b):
    bsz, seq, d = x.shape
    n = bsz * seq
    n_slots = n * TOP_K + N_EXPERTS * MOE_BLOCK
    n_blocks = n_slots // MOE_BLOCK

    gate_pad = jnp.zeros(w_in.shape[:2] + (LANES - N_HEADS,), w_in.dtype)
    w_all = jnp.concatenate(
        [w_in[..., OFF_Q:OFF_I], w_in[..., OFF_B:P_IN], w_in[..., OFF_I:OFF_F], gate_pad,
         w_in[..., OFF_F:OFF_B], gate_pad], axis=-1).astype(BF16)
    b_all = jnp.concatenate(
        [b_in[..., OFF_Q:OFF_I], b_in[..., OFF_B:P_IN], b_in[..., OFF_I:OFF_F], gate_pad[:, 0, :],
         b_in[..., OFF_F:OFF_B], gate_pad[:, 0, :]], axis=-1).astype(F32)[:, None, :]
    w_out_b = w_out.astype(BF16)
    rw_t = router_w.T.astype(F32)
    rw_hi, rw_lo = _split_bf16(rw_t)
    rb = router_b.astype(F32)[:, None]
    wgu = jnp.concatenate([w_gate, w_up], axis=-1).astype(BF16)
    wd = w_down.astype(BF16)

    xf = x.reshape(n, d)
    for l in range(DEPTH):
        x1, eidx, gates, rank, cnt = _mixer_call(
            xf, seq, w_all[l], b_all[l], conv_w[l], mh_norm_g[l][None, :], w_out_b[l],
            ln_mix_g[l][None, :], ln_mix_b[l][None, :], rw_hi, rw_lo, rb)
        dest, block_expert, n_valid, block_rows = _slot_plan(cnt[:, 0], eidx, rank, n_blocks)
        buf = _sc_scatter_pair(x1, dest, n_slots)
        y_buf = _expert_call(block_expert, n_valid, block_rows, buf, wgu[l], wd[l])
        xf = _combine_call(x1, dest, gates.T, y_buf, ln_moe_g[l][None, :], ln_moe_b[l][None, :])
    return xf.reshape(bsz, seq, d)
```

```python
import functools

import jax
import jax.numpy as jnp
from jax import lax
from jax.experimental import pallas as pl
from jax.experimental.pallas import tpu as pltpu
from jax.experimental.pallas import tpu_sc as plsc

D_MODEL = 1024
DEPTH = 4
D_MLSTM = D_MODEL // 2
D_CONV = D_MODEL - D_MLSTM
N_HEADS = 4
HEAD_DIM = D_MLSTM // N_HEADS
CONV_WIDTH = 3
CHUNK = 128
N_EXPERTS = 16
N_GROUPS = 4
EXPERTS_PER_GROUP = N_EXPERTS // N_GROUPS
TOP_K = 2
D_FF = D_MODEL // 2
LN_EPS = 1e-5
ALPHA = (2 * DEPTH) ** 0.25

OFF_Q = 0
OFF_K = OFF_Q + D_MLSTM
OFF_V = OFF_K + D_MLSTM
OFF_O = OFF_V + D_MLSTM
OFF_I = OFF_O + D_MLSTM
OFF_F = OFF_I + N_HEADS
OFF_B = OFF_F + N_HEADS
OFF_C = OFF_B + D_CONV
OFF_X = OFF_C + D_CONV
P_IN = OFF_X + D_CONV

LANES = 128
W_M = 4 * D_MLSTM
W_C = 3 * D_CONV
GATE_I = W_M + W_C
GATE_F = GATE_I + LANES
W_ALL = GATE_F + LANES

MIX_ROWS = 1024
MOE_BLOCK = 512
ROW_TILE = 256
SC_WINDOW = 64
VMEM_LIMIT = 56 * 1024 * 1024

F32 = jnp.float32
BF16 = jnp.bfloat16


def _dot(a, b):
    return jnp.dot(a, b, preferred_element_type=F32)


def _dot_nt(a, b):
    return lax.dot_general(a, b, (((1,), (1,)), ((), ())), preferred_element_type=F32)


def _split_bf16(x):
    hi = x.astype(BF16)
    lo = (x - hi.astype(F32)).astype(BF16)
    return hi, lo


def _pack_bf16(x):
    c = x.shape[1] // 2
    bits = lax.bitcast_convert_type(x.astype(BF16).astype(F32), jnp.uint32)
    return (bits[:, c:] & jnp.uint32(0xFFFF0000)) | (bits[:, :c] >> 16)


def _unpack_bf16(w):
    lo = lax.bitcast_convert_type(w << 16, F32)
    hi = lax.bitcast_convert_type(w & jnp.uint32(0xFFFF0000), F32)
    return jnp.concatenate([lo, hi], axis=1)


def _layer_norm(z, g, b):
    mu = jnp.mean(z, axis=-1, keepdims=True)
    d = z - mu
    var = jnp.mean(d * d, axis=-1, keepdims=True)
    return d * lax.rsqrt(var + LN_EPS) * g + b


def _rows(x, c):
    return x[c * CHUNK:(c + 1) * CHUNK]


def _stack(parts):
    return parts[0] if len(parts) == 1 else jnp.concatenate(parts, axis=0)


def _mlstm_tile(p_ref, y_ref, state_ref, m_ref, mhg_ref, fresh, ts):
    L = CHUNK
    nch = ts // L
    row_i = lax.broadcasted_iota(jnp.int32, (L, L), 0)
    col_i = lax.broadcasted_iota(jnp.int32, (L, L), 1)
    tri = (col_i <= row_i).astype(BF16)
    causal = _stack([col_i <= row_i] * nch)
    ones = jnp.ones((ts, LANES), BF16)

    def per_chunk(vals):
        return _stack([jnp.broadcast_to(vals[c], (L, LANES)) for c in range(nch)])

    ig = p_ref[:, pl.ds(GATE_I, LANES)]
    fg = p_ref[:, pl.ds(GATE_F, LANES)]
    logf = jnp.minimum(fg, 0.0) - jnp.log1p(jnp.exp(-jnp.abs(fg)))
    lf_hi, lf_lo = _split_bf16(logf)
    a = _stack([_dot(tri, _rows(lf_hi, c)) + _dot(tri, _rows(lf_lo, c)) for c in range(nch)])
    b_t = [(_rows(ig, c) - _rows(a, c)).T for c in range(nch)]

    for h in range(N_HEADS):
        c0 = h * HEAD_DIM
        q = p_ref[:, pl.ds(OFF_Q + c0, HEAD_DIM)]
        k = p_ref[:, pl.ds(OFF_K + c0, HEAD_DIM)] * (HEAD_DIM ** -0.5)
        v = p_ref[:, pl.ds(OFF_V + c0, HEAD_DIM)]
        o = p_ref[:, pl.ds(OFF_O + c0, HEAD_DIM)]

        a_b = jnp.broadcast_to(a[:, h:h + 1], (ts, LANES))
        ig_b = jnp.broadcast_to(ig[:, h:h + 1], (ts, LANES))
        g = [a_b[(c + 1) * L - 1:(c + 1) * L, :] for c in range(nch)]
        w_end = per_chunk(g) - a_b + ig_b
        m_loc = [jnp.max(_rows(w_end, c), axis=0, keepdims=True) for c in range(nch)]
        m = [jnp.where(fresh, 0.0, m_ref[h:h + 1, :])]
        for c in range(nch):
            m.append(jnp.maximum(g[c] + m[c], m_loc[c]))
        m_ref[h:h + 1, :] = m[nch]
        log_inter = a_b + per_chunk(m)
        e_end = jnp.exp(w_end - per_chunk(m_loc))

        b_rows = _stack([jnp.broadcast_to(b_t[c][h:h + 1, :], (L, L)) for c in range(nch)])
        log_d = jnp.where(causal, a_b + b_rows, -jnp.inf)
        m_out = jnp.maximum(log_inter, jnp.max(log_d, axis=-1, keepdims=True))
        dmat = jnp.exp(log_d - m_out)
        qb = q.astype(BF16)
        kb = k.astype(BF16)
        v_aug = jnp.concatenate([v.astype(BF16), ones], axis=1)
        s_qk = (_stack([_dot_nt(_rows(qb, c), _rows(kb, c)) for c in range(nch)]) * dmat).astype(BF16)
        intra = _stack([_dot(_rows(s_qk, c), _rows(v_aug, c)) for c in range(nch)])
        k_e = k * e_end
        s_loc = [_dot(_rows(k_e, c).T.astype(BF16), _rows(v_aug, c)) for c in range(nch)]
        state = [jnp.where(fresh, 0.0, state_ref[h])]
        for c in range(nch):
            keep = jnp.exp(g[c] + m[c] - m[c + 1])
            take = jnp.exp(m_loc[c] - m[c + 1])
            state.append(jnp.concatenate([keep, keep], axis=1) * state[c]
                         + jnp.concatenate([take, take], axis=1) * s_loc[c])
        state_ref[h] = state[nch]
        inter = _stack([_dot(_rows(qb, c), state[c].astype(BF16)) for c in range(nch)])
        e_inter = jnp.exp(log_inter - m_out)
        num = intra[:, :HEAD_DIM] + e_inter * inter[:, :HEAD_DIM]
        den = intra[:, HEAD_DIM:] + e_inter * inter[:, HEAD_DIM:]
        hh = num / jnp.maximum(jnp.abs(den), jnp.exp(-m_out))

        mu = jnp.mean(hh, axis=-1, keepdims=True)
        dh = hh - mu
        var = jnp.mean(dh * dh, axis=-1, keepdims=True)
        hn = dh * lax.rsqrt(var + LN_EPS) * mhg_ref[:, pl.ds(c0, HEAD_DIM)]
        y_ref[:, pl.ds(c0, HEAD_DIM)] = (jax.nn.sigmoid(o) * hn).astype(y_ref.dtype)


def _route(logits_t, ts):
    lg = [logits_t[e:e + 1, :] for e in range(N_EXPERTS)]
    mx = functools.reduce(jnp.maximum, lg)
    ex = [jnp.exp(v - mx) for v in lg]
    tot = functools.reduce(lambda a, b: a + b, ex)
    p = [v / tot for v in ex]
    scores = []
    for g in range(N_GROUPS):
        p0, p1, p2, p3 = p[g * EXPERTS_PER_GROUP:(g + 1) * EXPERTS_PER_GROUP]
        hi01, lo01 = jnp.maximum(p0, p1), jnp.minimum(p0, p1)
        hi23, lo23 = jnp.maximum(p2, p3), jnp.minimum(p2, p3)
        top1 = jnp.maximum(hi01, hi23)
        top2 = jnp.maximum(jnp.minimum(hi01, hi23), jnp.maximum(lo01, lo23))
        scores.append(top1 + top2)
    grp = jnp.zeros((1, ts), jnp.int32)
    best = scores[0]
    for g in range(1, N_GROUPS):
        upd = scores[g] > best
        best = jnp.where(upd, scores[g], best)
        grp = jnp.where(upd, g, grp)
    sel = []
    for j in range(EXPERTS_PER_GROUP):
        v = p[j]
        for g in range(1, N_GROUPS):
            v = jnp.where(grp == g, p[g * EXPERTS_PER_GROUP + j], v)
        sel.append(v)
    i1 = jnp.zeros((1, ts), jnp.int32)
    v1 = sel[0]
    for j in range(1, EXPERTS_PER_GROUP):
        upd = sel[j] > v1
        v1 = jnp.where(upd, sel[j], v1)
        i1 = jnp.where(upd, j, i1)
    i2 = jnp.zeros((1, ts), jnp.int32)
    v2 = jnp.full((1, ts), -jnp.inf, F32)
    for j in range(EXPERTS_PER_GROUP):
        cand = jnp.where(i1 == j, -jnp.inf, sel[j])
        upd = cand > v2
        v2 = jnp.where(upd, cand, v2)
        i2 = jnp.where(upd, j, i2)
    e0 = grp * EXPERTS_PER_GROUP + i1
    e1 = grp * EXPERTS_PER_GROUP + i2
    s12 = v1 + v2
    return e0, e1, v1 / s12, v2 / s12


def _in_proj(x, w_ref, b_ref, p_ref):
    p_ref[...] = _dot(x.astype(BF16), w_ref[...]) + b_ref[...]


def _mix_tile(p_ref, x, r0, fresh, first_tile, refs, ts):
    (cw_ref, mhg_ref, wo_ref, lng_ref, lnb_ref, rwh_ref, rwl_ref, rb_ref,
     xo_ref, xp_ref, eidx_ref, gate_ref, rank_ref, cnt_ref,
     y_ref, state_ref, m_ref, ucarry_ref, cnt_scr) = refs
    _mlstm_tile(p_ref, y_ref, state_ref, m_ref, mhg_ref, fresh, ts)

    gate_b = p_ref[:, pl.ds(W_M, D_CONV)]
    u = p_ref[:, pl.ds(W_M + D_CONV, D_CONV)] * p_ref[:, pl.ds(W_M + 2 * D_CONV, D_CONV)]
    rid = lax.broadcasted_iota(jnp.int32, (ts, D_CONV), 0)
    prev1 = jnp.where(fresh, 0.0, ucarry_ref[7:8, :])
    prev2 = jnp.where(fresh, 0.0, ucarry_ref[6:7, :])
    u1 = jnp.where(rid == 0, prev1, pltpu.roll(u, 1, 0))
    u2 = jnp.where(rid == 0, prev2, jnp.where(rid == 1, prev1, pltpu.roll(u, 2, 0)))
    conv = cw_ref[0:1, :] * u2 + cw_ref[1:2, :] * u1 + cw_ref[2:3, :] * u
    y_ref[:, pl.ds(D_MLSTM, D_CONV)] = (gate_b * conv).astype(y_ref.dtype)
    ucarry_ref[...] = u[ts - 8:ts, :]

    mix = _dot(y_ref[...], wo_ref[...])
    x1 = _layer_norm(ALPHA * x + mix, lng_ref[...], lnb_ref[...])
    xo_ref[pl.ds(r0, ts), :] = x1
    xp_ref[pl.ds(r0, ts), :] = _pack_bf16(x1)

    x_hi, x_lo = _split_bf16(x1)
    logits_t = (_dot_nt(rwh_ref[...], x_hi) + _dot_nt(rwl_ref[...], x_hi)
                + _dot_nt(rwh_ref[...], x_lo) + rb_ref[...])
    e0, e1, g0, g1 = _route(logits_t, ts)
    eidx_ref[0:1, pl.ds(r0, ts)] = e0
    eidx_ref[1:2, pl.ds(r0, ts)] = e1
    gate_ref[0:1, pl.ds(r0, ts)] = g0
    gate_ref[1:2, pl.ds(r0, ts)] = g1

    eid = lax.broadcasted_iota(jnp.int32, (N_EXPERTS, ts), 0)
    hit0 = eid == e0
    hit1 = eid == e1
    onehot = hit0.astype(F32) + hit1.astype(F32)
    t_r = lax.broadcasted_iota(jnp.int32, (ts, ts), 0)
    t_c = lax.broadcasted_iota(jnp.int32, (ts, ts), 1)
    upper = (t_r <= t_c).astype(BF16)
    csum = _dot(onehot.astype(BF16), upper)
    seen = jnp.where(first_tile, 0.0, cnt_scr[...])
    before = csum - onehot + seen[:, 0:1]
    rank_ref[0:1, pl.ds(r0, ts)] = jnp.sum(jnp.where(hit0, before, 0.0), axis=0, keepdims=True).astype(jnp.int32)
    rank_ref[1:2, pl.ds(r0, ts)] = jnp.sum(jnp.where(hit1, before, 0.0), axis=0, keepdims=True).astype(jnp.int32)
    seen = seen + csum[:, ts - 1:ts]
    cnt_scr[...] = seen
    cnt_ref[...] = seen.astype(jnp.int32)


def _mixer_kernel(x_ref, w_ref, b_ref, cw_ref, mhg_ref, wo_ref, lng_ref, lnb_ref,
                  rwh_ref, rwl_ref, rb_ref,
                  xo_ref, xp_ref, eidx_ref, gate_ref, rank_ref, cnt_ref,
                  p_ref, y_ref, state_ref, m_ref, ucarry_ref, cnt_scr, *, ts, seq):
    step = pl.program_id(0)
    refs = (cw_ref, mhg_ref, wo_ref, lng_ref, lnb_ref, rwh_ref, rwl_ref, rb_ref,
            xo_ref, xp_ref, eidx_ref, gate_ref, rank_ref, cnt_ref,
            y_ref, state_ref, m_ref, ucarry_ref, cnt_scr)
    _in_proj(x_ref[...], w_ref, b_ref, p_ref)
    _mix_tile(p_ref, x_ref[...], 0, (step * ts) % seq == 0, step == 0, refs, ts)


def _const_spec(shape):
    nd = len(shape)
    return pl.BlockSpec(shape, lambda *_: (0,) * nd, pipeline_mode=pl.Buffered(1))


def _mixer_call(x_flat, seq, w_all, b_all, conv_w, mh_g, w_out, ln_g, ln_b, rw_hi, rw_lo, rb):
    n, d = x_flat.shape
    ts = min(MIX_ROWS, seq // 2)
    tok_spec = pl.BlockSpec((TOP_K, ts), lambda i: (0, i))
    return pl.pallas_call(
        functools.partial(_mixer_kernel, ts=ts, seq=seq),
        grid=(n // ts,),
        in_specs=[
            pl.BlockSpec((ts, d), lambda i: (i, 0)),
            _const_spec(w_all.shape), _const_spec(b_all.shape), _const_spec(conv_w.shape),
            _const_spec(mh_g.shape), _const_spec(w_out.shape), _const_spec(ln_g.shape),
            _const_spec(ln_b.shape), _const_spec(rw_hi.shape), _const_spec(rw_lo.shape),
            _const_spec(rb.shape),
        ],
        out_specs=[
            pl.BlockSpec((ts, d), lambda i: (i, 0)),
            pl.BlockSpec((ts, d // 2), lambda i: (i, 0)),
            tok_spec, tok_spec, tok_spec,
            pl.BlockSpec((N_EXPERTS, LANES), lambda i: (0, 0)),
        ],
        out_shape=[
            jax.ShapeDtypeStruct((n, d), F32),
            jax.ShapeDtypeStruct((n, d // 2), jnp.uint32),
            jax.ShapeDtypeStruct((TOP_K, n), jnp.int32),
            jax.ShapeDtypeStruct((TOP_K, n), F32),
            jax.ShapeDtypeStruct((TOP_K, n), jnp.int32),
            jax.ShapeDtypeStruct((N_EXPERTS, LANES), jnp.int32),
        ],
        scratch_shapes=[
            pltpu.VMEM((ts, W_ALL), F32),
            pltpu.VMEM((ts, D_MODEL), BF16),
            pltpu.VMEM((N_HEADS, HEAD_DIM, 2 * HEAD_DIM), F32),
            pltpu.VMEM((8, LANES), F32),
            pltpu.VMEM((8, D_CONV), F32),
            pltpu.VMEM((N_EXPERTS, LANES), F32),
        ],
        compiler_params=pltpu.CompilerParams(
            dimension_semantics=("arbitrary",), vmem_limit_bytes=VMEM_LIMIT),
        name="mixer",
    )(x_flat, w_all, b_all, conv_w, mh_g, w_out, ln_g, ln_b, rw_hi, rw_lo, rb)


def _sc_mesh():
    return plsc.VectorSubcoreMesh(core_axis_name="core", subcore_axis_name="subcore")


def _sc_scatter_pair(x_flat, dest, n_slots):
    n, d = x_flat.shape
    windows = dest.reshape(TOP_K, n // SC_WINDOW, SC_WINDOW).transpose(1, 0, 2)

    @functools.partial(
        pl.kernel, out_type=jax.ShapeDtypeStruct((n_slots, d), x_flat.dtype), mesh=_sc_mesh(),
        scratch_types=[])
    def scatter_kernel(x_hbm, i_hbm, o_hbm):
        def body(x_vmem, i_vmem):
            for k in range(TOP_K):
                pltpu.sync_copy(x_vmem, o_hbm.at[i_vmem.at[0, k]])

        pltpu.emit_pipeline(
            body,
            grid=(n // SC_WINDOW,),
            in_specs=[pl.BlockSpec((SC_WINDOW, d), lambda i: (i, 0)),
                      pl.BlockSpec((1, TOP_K, SC_WINDOW), lambda i: (i, 0, 0))],
            out_specs=[],
            core_axis_name=("core", "subcore"),
            dimension_semantics=(pltpu.PARALLEL,),
        )(x_hbm, i_hbm)

    return scatter_kernel(x_flat, windows)


def _expert_kernel(be_ref, nv_ref, br_ref, buf_ref, wgu_ref, wd_ref, y_ref):
    used = pl.program_id(0) < nv_ref[0]

    @pl.when(used)
    def _():
        rid = lax.broadcasted_iota(jnp.int32, (MOE_BLOCK, 1), 0)
        words = jnp.where(rid < br_ref[pl.program_id(0)], buf_ref[...], jnp.uint32(0))
        xb = _unpack_bf16(words).astype(BF16)
        h = _dot(xb, wgu_ref[0])
        hdn = jax.nn.silu(h[:, :D_FF]) * h[:, D_FF:]
        y_ref[...] = _pack_bf16(_dot(hdn.astype(BF16), wd_ref[0]))

    @pl.when(jnp.logical_not(used))
    def _():
        y_ref[...] = jnp.zeros_like(y_ref)


def _expert_call(block_expert, n_valid, block_rows, buf, wgu, wd):
    n_slots, dw = buf.shape
    d = 2 * dw
    n_blocks = n_slots // MOE_BLOCK

    def row_map(i, be, nv, br):
        return (jnp.minimum(i, nv[0] - 1), 0)

    def w_map(i, be, nv, br):
        return (be[jnp.minimum(i, nv[0] - 1)], 0, 0)

    return pl.pallas_call(
        _expert_kernel,
        grid_spec=pltpu.PrefetchScalarGridSpec(
            num_scalar_prefetch=3,
            grid=(n_blocks,),
            in_specs=[
                pl.BlockSpec((MOE_BLOCK, dw), row_map),
                pl.BlockSpec((1, d, 2 * D_FF), w_map),
                pl.BlockSpec((1, D_FF, d), w_map),
            ],
            out_specs=pl.BlockSpec((MOE_BLOCK, dw), lambda i, be, nv, br: (i, 0)),
        ),
        out_shape=jax.ShapeDtypeStruct((n_slots, dw), jnp.uint32),
        compiler_params=pltpu.CompilerParams(
            dimension_semantics=("arbitrary",), vmem_limit_bytes=VMEM_LIMIT),
        name="experts",
    )(block_expert, n_valid, block_rows, buf, wgu, wd)


def _sc_gather(table, idx):
    m = idx.shape[0]
    d = table.shape[1]

    @functools.partial(
        pl.kernel, out_type=jax.ShapeDtypeStruct((m, d), table.dtype), mesh=_sc_mesh(),
        scratch_types=[])
    def gather_kernel(t_hbm, i_hbm, o_hbm):
        def body(i_vmem, o_vmem):
            pltpu.sync_copy(t_hbm.at[i_vmem.at[0, 0]], o_vmem)

        pltpu.emit_pipeline(
            body,
            grid=(m // SC_WINDOW,),
            in_specs=[pl.BlockSpec((1, 1, SC_WINDOW), lambda i: (i, 0, 0))],
            out_specs=[pl.BlockSpec((SC_WINDOW, d), lambda i: (i, 0))],
            core_axis_name=("core", "subcore"),
            dimension_semantics=(pltpu.PARALLEL,),
        )(i_hbm, o_hbm)

    return gather_kernel(table, idx.reshape(m // SC_WINDOW, 1, SC_WINDOW))


def _combine_kernel(x_ref, y0_ref, y1_ref, gate_ref, lng_ref, lnb_ref, o_ref):
    ffn = _unpack_bf16(y0_ref[...]) * gate_ref[:, 0:1] + _unpack_bf16(y1_ref[...]) * gate_ref[:, 1:2]
    o_ref[...] = _layer_norm(ALPHA * x_ref[...] + ffn, lng_ref[...], lnb_ref[...])


def _combine_call(x_flat, dest, gates_t, y_buf, ln_g, ln_b):
    n, d = x_flat.shape
    rows = min(2 * ROW_TILE, n)
    nb = n // rows
    yg = _sc_gather(y_buf, dest.reshape(TOP_K * n))
    return pl.pallas_call(
        _combine_kernel,
        grid=(nb,),
        in_specs=[
            pl.BlockSpec((rows, d), lambda i: (i, 0)),
            pl.BlockSpec((rows, d // 2), lambda i: (i, 0)),
            pl.BlockSpec((rows, d // 2), lambda i: (i + nb, 0)),
            pl.BlockSpec((rows, TOP_K), lambda i: (i, 0)),
            _const_spec(ln_g.shape), _const_spec(ln_b.shape),
        ],
        out_specs=pl.BlockSpec((rows, d), lambda i: (i, 0)),
        out_shape=jax.ShapeDtypeStruct((n, d), F32),
        compiler_params=pltpu.CompilerParams(
            dimension_semantics=("arbitrary",), vmem_limit_bytes=VMEM_LIMIT),
        name="combine",
    )(x_flat, yg, yg, gates_t, ln_g, ln_b)


def _slot_plan(counts, eidx, rank, n_blocks):
    padded = (counts + MOE_BLOCK - 1) // MOE_BLOCK * MOE_BLOCK
    pad_end = jnp.cumsum(padded)
    pad_start = pad_end - padded
    dest = rank
    for e in range(N_EXPERTS):
        dest = dest + jnp.where(eidx == e, pad_start[e], 0)
    block_row0 = jnp.arange(n_blocks, dtype=jnp.int32) * MOE_BLOCK
    block_expert = jnp.minimum(
        jnp.sum(pad_end[None, :] <= block_row0[:, None], axis=1), N_EXPERTS - 1).astype(jnp.int32)
    n_valid = (pad_end[-1:] // MOE_BLOCK).astype(jnp.int32)
    block_rows = jnp.clip(
        (pad_start + counts)[block_expert] - block_row0, 0, MOE_BLOCK).astype(jnp.int32)
    return dest.astype(jnp.int32), block_expert, n_valid, block_rows


def kernel(x, w_in, b_in, conv_w, mh_norm_g, w_out, ln_mix_g, ln_mix_b, router_w, router_b,
           w_gate, w_up, w_down, ln_moe_g, ln_moe_b):
    bsz, seq, d = x.shape
    n = bsz * seq
    n_slots = n * TOP_K + N_EXPERTS * MOE_BLOCK
    n_blocks = n_slots // MOE_BLOCK

    gate_pad = jnp.zeros(w_in.shape[:2] + (LANES - N_HEADS,), w_in.dtype)
    w_all = jnp.concatenate(
        [w_in[..., OFF_Q:OFF_I], w_in[..., OFF_B:P_IN], w_in[..., OFF_I:OFF_F], gate_pad,
         w_in[..., OFF_F:OFF_B], gate_pad], axis=-1).astype(BF16)
    b_all = jnp.concatenate(
        [b_in[..., OFF_Q:OFF_I], b_in[..., OFF_B:P_IN], b_in[..., OFF_I:OFF_F], gate_pad[:, 0, :],
         b_in[..., OFF_F:OFF_B], gate_pad[:, 0, :]], axis=-1).astype(F32)[:, None, :]
    w_out_b = w_out.astype(BF16)
    rw_t = router_w.T.astype(F32)
    rw_hi, rw_lo = _split_bf16(rw_t)
    rb = router_b.astype(F32)[:, None]
    wgu = jnp.concatenate([w_gate, w_up], axis=-1).astype(BF16)
    wd = w_down.astype(BF16)

    xf = x.reshape(n, d)
    for l in range(DEPTH):
        x1, x1p, eidx, gates, rank, cnt = _mixer_call(
            xf, seq, w_all[l], b_all[l], conv_w[l], mh_norm_g[l][None, :], w_out_b[l],
            ln_mix_g[l][None, :], ln_mix_b[l][None, :], rw_hi, rw_lo, rb)
        dest, block_expert, n_valid, block_rows = _slot_plan(cnt[:, 0], eidx, rank, n_blocks)
        buf = _sc_scatter_pair(x1p, dest, n_slots)
        y_buf = _expert_call(block_expert, n_valid, block_rows, buf, wgu[l], wd[l])
        xf = _combine_call(x1, dest, gates.T, y_buf, ln_moe_g[l][None, :], ln_moe_b[l][None, :])
    return xf.reshape(bsz, seq, d)
```

```python
import functools

import jax
import jax.numpy as jnp
from jax import lax
from jax.experimental import pallas as pl
from jax.experimental.pallas import tpu as pltpu
from jax.experimental.pallas import tpu_sc as plsc

D_MODEL = 1024
DEPTH = 4
D_MLSTM = D_MODEL // 2
D_CONV = D_MODEL - D_MLSTM
N_HEADS = 4
HEAD_DIM = D_MLSTM // N_HEADS
CONV_WIDTH = 3
CHUNK = 128
N_EXPERTS = 16
N_GROUPS = 4
EXPERTS_PER_GROUP = N_EXPERTS // N_GROUPS
TOP_K = 2
D_FF = D_MODEL // 2
LN_EPS = 1e-5
ALPHA = (2 * DEPTH) ** 0.25

OFF_Q = 0
OFF_K = OFF_Q + D_MLSTM
OFF_V = OFF_K + D_MLSTM
OFF_O = OFF_V + D_MLSTM
OFF_I = OFF_O + D_MLSTM
OFF_F = OFF_I + N_HEADS
OFF_B = OFF_F + N_HEADS
OFF_C = OFF_B + D_CONV
OFF_X = OFF_C + D_CONV
P_IN = OFF_X + D_CONV

LANES = 128
W_M = 4 * D_MLSTM
W_C = 3 * D_CONV
GATE_I = W_M + W_C
GATE_F = GATE_I + LANES
W_ALL = GATE_F + LANES

MIX_ROWS = 512
PIECE_COLS = 256
MOE_BLOCK = 512
ROW_TILE = 256
SC_WINDOW = 64
VMEM_LIMIT = 56 * 1024 * 1024

F32 = jnp.float32
BF16 = jnp.bfloat16


def _dot(a, b):
    return jnp.dot(a, b, preferred_element_type=F32)


def _dot_nt(a, b):
    return lax.dot_general(a, b, (((1,), (1,)), ((), ())), preferred_element_type=F32)


def _split_bf16(x):
    hi = x.astype(BF16)
    lo = (x - hi.astype(F32)).astype(BF16)
    return hi, lo


def _pack_bf16(x):
    c = x.shape[1] // 2
    bits = lax.bitcast_convert_type(x.astype(BF16).astype(F32), jnp.uint32)
    return (bits[:, c:] & jnp.uint32(0xFFFF0000)) | (bits[:, :c] >> 16)


def _unpack_bf16(w):
    lo = lax.bitcast_convert_type(w << 16, F32)
    hi = lax.bitcast_convert_type(w & jnp.uint32(0xFFFF0000), F32)
    return jnp.concatenate([lo, hi], axis=1)


def _layer_norm(z, g, b):
    mu = jnp.mean(z, axis=-1, keepdims=True)
    d = z - mu
    var = jnp.mean(d * d, axis=-1, keepdims=True)
    return d * lax.rsqrt(var + LN_EPS) * g + b


def _rows(x, c):
    return x[c * CHUNK:(c + 1) * CHUNK]


def _stack(parts):
    return parts[0] if len(parts) == 1 else jnp.concatenate(parts, axis=0)


def _mlstm_tile(p_ref, y_ref, state_ref, m_ref, mhg_ref, fresh, ts, fill):
    L = CHUNK
    nch = ts // L
    row_i = lax.broadcasted_iota(jnp.int32, (L, L), 0)
    col_i = lax.broadcasted_iota(jnp.int32, (L, L), 1)
    tri = (col_i <= row_i).astype(BF16)
    causal = _stack([col_i <= row_i] * nch)
    ones = jnp.ones((ts, LANES), BF16)

    def per_chunk(vals):
        return _stack([jnp.broadcast_to(vals[c], (L, LANES)) for c in range(nch)])

    ig = p_ref[:, pl.ds(GATE_I, LANES)]
    fg = p_ref[:, pl.ds(GATE_F, LANES)]
    logf = jnp.minimum(fg, 0.0) - jnp.log1p(jnp.exp(-jnp.abs(fg)))
    lf_hi, lf_lo = _split_bf16(logf)
    a = _stack([_dot(tri, _rows(lf_hi, c)) + _dot(tri, _rows(lf_lo, c)) for c in range(nch)])
    b_t = [(_rows(ig, c) - _rows(a, c)).T for c in range(nch)]

    for h in range(N_HEADS):
        c0 = h * HEAD_DIM
        q = p_ref[:, pl.ds(OFF_Q + c0, HEAD_DIM)]
        k = p_ref[:, pl.ds(OFF_K + c0, HEAD_DIM)] * (HEAD_DIM ** -0.5)
        v = p_ref[:, pl.ds(OFF_V + c0, HEAD_DIM)]
        o = p_ref[:, pl.ds(OFF_O + c0, HEAD_DIM)]

        a_b = jnp.broadcast_to(a[:, h:h + 1], (ts, LANES))
        ig_b = jnp.broadcast_to(ig[:, h:h + 1], (ts, LANES))
        g = [a_b[(c + 1) * L - 1:(c + 1) * L, :] for c in range(nch)]
        w_end = per_chunk(g) - a_b + ig_b
        m_loc = [jnp.max(_rows(w_end, c), axis=0, keepdims=True) for c in range(nch)]
        m = [jnp.where(fresh, 0.0, m_ref[h:h + 1, :])]
        for c in range(nch):
            m.append(jnp.maximum(g[c] + m[c], m_loc[c]))
        m_ref[h:h + 1, :] = m[nch]
        log_inter = a_b + per_chunk(m)
        e_end = jnp.exp(w_end - per_chunk(m_loc))

        b_rows = _stack([jnp.broadcast_to(b_t[c][h:h + 1, :], (L, L)) for c in range(nch)])
        log_d = jnp.where(causal, a_b + b_rows, -jnp.inf)
        m_out = jnp.maximum(log_inter, jnp.max(log_d, axis=-1, keepdims=True))
        dmat = jnp.exp(log_d - m_out)
        fill()
        qb = q.astype(BF16)
        kb = k.astype(BF16)
        v_aug = jnp.concatenate([v.astype(BF16), ones], axis=1)
        s_qk = (_stack([_dot_nt(_rows(qb, c), _rows(kb, c)) for c in range(nch)]) * dmat).astype(BF16)
        intra = _stack([_dot(_rows(s_qk, c), _rows(v_aug, c)) for c in range(nch)])
        k_e = k * e_end
        s_loc = [_dot(_rows(k_e, c).T.astype(BF16), _rows(v_aug, c)) for c in range(nch)]
        state = [jnp.where(fresh, 0.0, state_ref[h])]
        for c in range(nch):
            keep = jnp.exp(g[c] + m[c] - m[c + 1])
            take = jnp.exp(m_loc[c] - m[c + 1])
            state.append(jnp.concatenate([keep, keep], axis=1) * state[c]
                         + jnp.concatenate([take, take], axis=1) * s_loc[c])
        state_ref[h] = state[nch]
        inter = _stack([_dot(_rows(qb, c), state[c].astype(BF16)) for c in range(nch)])
        e_inter = jnp.exp(log_inter - m_out)
        num = intra[:, :HEAD_DIM] + e_inter * inter[:, :HEAD_DIM]
        den = intra[:, HEAD_DIM:] + e_inter * inter[:, HEAD_DIM:]
        hh = num / jnp.maximum(jnp.abs(den), jnp.exp(-m_out))

        mu = jnp.mean(hh, axis=-1, keepdims=True)
        dh = hh - mu
        var = jnp.mean(dh * dh, axis=-1, keepdims=True)
        hn = dh * lax.rsqrt(var + LN_EPS) * mhg_ref[:, pl.ds(c0, HEAD_DIM)]
        y_ref[:, pl.ds(c0, HEAD_DIM)] = (jax.nn.sigmoid(o) * hn).astype(y_ref.dtype)
        fill()


def _route(logits_t, ts):
    lg = [logits_t[e:e + 1, :] for e in range(N_EXPERTS)]
    mx = functools.reduce(jnp.maximum, lg)
    ex = [jnp.exp(v - mx) for v in lg]
    tot = functools.reduce(lambda a, b: a + b, ex)
    p = [v / tot for v in ex]
    scores = []
    for g in range(N_GROUPS):
        p0, p1, p2, p3 = p[g * EXPERTS_PER_GROUP:(g + 1) * EXPERTS_PER_GROUP]
        hi01, lo01 = jnp.maximum(p0, p1), jnp.minimum(p0, p1)
        hi23, lo23 = jnp.maximum(p2, p3), jnp.minimum(p2, p3)
        top1 = jnp.maximum(hi01, hi23)
        top2 = jnp.maximum(jnp.minimum(hi01, hi23), jnp.maximum(lo01, lo23))
        scores.append(top1 + top2)
    grp = jnp.zeros((1, ts), jnp.int32)
    best = scores[0]
    for g in range(1, N_GROUPS):
        upd = scores[g] > best
        best = jnp.where(upd, scores[g], best)
        grp = jnp.where(upd, g, grp)
    sel = []
    for j in range(EXPERTS_PER_GROUP):
        v = p[j]
        for g in range(1, N_GROUPS):
            v = jnp.where(grp == g, p[g * EXPERTS_PER_GROUP + j], v)
        sel.append(v)
    i1 = jnp.zeros((1, ts), jnp.int32)
    v1 = sel[0]
    for j in range(1, EXPERTS_PER_GROUP):
        upd = sel[j] > v1
        v1 = jnp.where(upd, sel[j], v1)
        i1 = jnp.where(upd, j, i1)
    i2 = jnp.zeros((1, ts), jnp.int32)
    v2 = jnp.full((1, ts), -jnp.inf, F32)
    for j in range(EXPERTS_PER_GROUP):
        cand = jnp.where(i1 == j, -jnp.inf, sel[j])
        upd = cand > v2
        v2 = jnp.where(upd, cand, v2)
        i2 = jnp.where(upd, j, i2)
    e0 = grp * EXPERTS_PER_GROUP + i1
    e1 = grp * EXPERTS_PER_GROUP + i2
    s12 = v1 + v2
    return e0, e1, v1 / s12, v2 / s12


def _in_proj_pieces(x, w_ref, b_ref, p_ref):
    xb = x.astype(BF16)

    def piece(j):
        cols = pl.ds(j * PIECE_COLS, PIECE_COLS)
        p_ref[:, cols] = _dot(xb, w_ref[:, cols]) + b_ref[:, cols]

    return [functools.partial(piece, j) for j in range(W_ALL // PIECE_COLS)]


def _in_proj(x, w_ref, b_ref, p_ref):
    for piece in _in_proj_pieces(x, w_ref, b_ref, p_ref):
        piece()


class _Filler:
    def __init__(self, thunks):
        self._thunks = list(thunks)

    def __call__(self, count=1):
        for _ in range(count):
            if self._thunks:
                self._thunks.pop(0)()

    def drain(self):
        self(len(self._thunks))


def _mix_tile(p_ref, x, r0, fresh, first_tile, refs, ts, fill):
    (cw_ref, mhg_ref, wo_ref, lng_ref, lnb_ref, rwh_ref, rwl_ref, rb_ref,
     xo_ref, xp_ref, eidx_ref, gate_ref, rank_ref, cnt_ref,
     y_ref, state_ref, m_ref, ucarry_ref, cnt_scr) = refs
    _mlstm_tile(p_ref, y_ref, state_ref, m_ref, mhg_ref, fresh, ts, fill)

    gate_b = p_ref[:, pl.ds(W_M, D_CONV)]
    u = p_ref[:, pl.ds(W_M + D_CONV, D_CONV)] * p_ref[:, pl.ds(W_M + 2 * D_CONV, D_CONV)]
    rid = lax.broadcasted_iota(jnp.int32, (ts, D_CONV), 0)
    prev1 = jnp.where(fresh, 0.0, ucarry_ref[7:8, :])
    prev2 = jnp.where(fresh, 0.0, ucarry_ref[6:7, :])
    u1 = jnp.where(rid == 0, prev1, pltpu.roll(u, 1, 0))
    u2 = jnp.where(rid == 0, prev2, jnp.where(rid == 1, prev1, pltpu.roll(u, 2, 0)))
    conv = cw_ref[0:1, :] * u2 + cw_ref[1:2, :] * u1 + cw_ref[2:3, :] * u
    y_ref[:, pl.ds(D_MLSTM, D_CONV)] = (gate_b * conv).astype(y_ref.dtype)
    ucarry_ref[...] = u[ts - 8:ts, :]
    fill()

    mix = _dot(y_ref[...], wo_ref[...])
    x1 = _layer_norm(ALPHA * x + mix, lng_ref[...], lnb_ref[...])
    xo_ref[pl.ds(r0, ts), :] = x1
    xp_ref[pl.ds(r0, ts), :] = _pack_bf16(x1)
    fill(2)

    x_hi, x_lo = _split_bf16(x1)
    logits_t = (_dot_nt(rwh_ref[...], x_hi) + _dot_nt(rwl_ref[...], x_hi)
                + _dot_nt(rwh_ref[...], x_lo) + rb_ref[...])
    e0, e1, g0, g1 = _route(logits_t, ts)
    eidx_ref[0:1, pl.ds(r0, ts)] = e0
    eidx_ref[1:2, pl.ds(r0, ts)] = e1
    gate_ref[0:1, pl.ds(r0, ts)] = g0
    gate_ref[1:2, pl.ds(r0, ts)] = g1
    fill(2)

    eid = lax.broadcasted_iota(jnp.int32, (N_EXPERTS, ts), 0)
    hit0 = eid == e0
    hit1 = eid == e1
    onehot = hit0.astype(F32) + hit1.astype(F32)
    t_r = lax.broadcasted_iota(jnp.int32, (ts, ts), 0)
    t_c = lax.broadcasted_iota(jnp.int32, (ts, ts), 1)
    upper = (t_r <= t_c).astype(BF16)
    csum = _dot(onehot.astype(BF16), upper)
    seen = jnp.where(first_tile, 0.0, cnt_scr[...])
    before = csum - onehot + seen[:, 0:1]
    rank_ref[0:1, pl.ds(r0, ts)] = jnp.sum(jnp.where(hit0, before, 0.0), axis=0, keepdims=True).astype(jnp.int32)
    rank_ref[1:2, pl.ds(r0, ts)] = jnp.sum(jnp.where(hit1, before, 0.0), axis=0, keepdims=True).astype(jnp.int32)
    seen = seen + csum[:, ts - 1:ts]
    cnt_scr[...] = seen
    cnt_ref[...] = seen.astype(jnp.int32)


def _mixer_kernel(x_ref, xn_ref, w_ref, b_ref, cw_ref, mhg_ref, wo_ref, lng_ref, lnb_ref,
                  rwh_ref, rwl_ref, rb_ref,
                  xo_ref, xp_ref, eidx_ref, gate_ref, rank_ref, cnt_ref,
                  pa_ref, pb_ref, y_ref, state_ref, m_ref, ucarry_ref, cnt_scr, *, ts, seq):
    step = pl.program_id(0)
    refs = (cw_ref, mhg_ref, wo_ref, lng_ref, lnb_ref, rwh_ref, rwl_ref, rb_ref,
            xo_ref, xp_ref, eidx_ref, gate_ref, rank_ref, cnt_ref,
            y_ref, state_ref, m_ref, ucarry_ref, cnt_scr)

    @pl.when(step == 0)
    def _():
        _in_proj(x_ref[0:ts, :], w_ref, b_ref, pa_ref)

    tile_a = 2 * step
    fill = _Filler(_in_proj_pieces(x_ref[ts:2 * ts, :], w_ref, b_ref, pb_ref))
    _mix_tile(pa_ref, x_ref[0:ts, :], 0, (tile_a * ts) % seq == 0, tile_a == 0, refs, ts, fill)
    fill.drain()
    fill = _Filler(_in_proj_pieces(xn_ref[...], w_ref, b_ref, pa_ref))
    _mix_tile(pb_ref, x_ref[ts:2 * ts, :], ts, ((tile_a + 1) * ts) % seq == 0, False, refs, ts, fill)
    fill.drain()


def _const_spec(shape):
    nd = len(shape)
    return pl.BlockSpec(shape, lambda *_: (0,) * nd, pipeline_mode=pl.Buffered(1))


def _mixer_call(x_flat, seq, w_all, b_all, conv_w, mh_g, w_out, ln_g, ln_b, rw_hi, rw_lo, rb):
    n, d = x_flat.shape
    ts = min(MIX_ROWS, seq // 2)
    n_steps = n // (2 * ts)
    tok_spec = pl.BlockSpec((TOP_K, 2 * ts), lambda i: (0, i))
    return pl.pallas_call(
        functools.partial(_mixer_kernel, ts=ts, seq=seq),
        grid=(n_steps,),
        in_specs=[
            pl.BlockSpec((2 * ts, d), lambda i: (i, 0)),
            pl.BlockSpec((ts, d), lambda i: (2 * jnp.minimum(i + 1, n_steps - 1), 0)),
            _const_spec(w_all.shape), _const_spec(b_all.shape), _const_spec(conv_w.shape),
            _const_spec(mh_g.shape), _const_spec(w_out.shape), _const_spec(ln_g.shape),
            _const_spec(ln_b.shape), _const_spec(rw_hi.shape), _const_spec(rw_lo.shape),
            _const_spec(rb.shape),
        ],
        out_specs=[
            pl.BlockSpec((2 * ts, d), lambda i: (i, 0)),
            pl.BlockSpec((2 * ts, d // 2), lambda i: (i, 0)),
            tok_spec, tok_spec, tok_spec,
            pl.BlockSpec((N_EXPERTS, LANES), lambda i: (0, 0)),
        ],
        out_shape=[
            jax.ShapeDtypeStruct((n, d), F32),
            jax.ShapeDtypeStruct((n, d // 2), jnp.uint32),
            jax.ShapeDtypeStruct((TOP_K, n), jnp.int32),
            jax.ShapeDtypeStruct((TOP_K, n), F32),
            jax.ShapeDtypeStruct((TOP_K, n), jnp.int32),
            jax.ShapeDtypeStruct((N_EXPERTS, LANES), jnp.int32),
        ],
        scratch_shapes=[
            pltpu.VMEM((ts, W_ALL), F32),
            pltpu.VMEM((ts, W_ALL), F32),
            pltpu.VMEM((ts, D_MODEL), BF16),
            pltpu.VMEM((N_HEADS, HEAD_DIM, 2 * HEAD_DIM), F32),
            pltpu.VMEM((8, LANES), F32),
            pltpu.VMEM((8, D_CONV), F32),
            pltpu.VMEM((N_EXPERTS, LANES), F32),
        ],
        compiler_params=pltpu.CompilerParams(
            dimension_semantics=("arbitrary",), vmem_limit_bytes=VMEM_LIMIT),
        name="mixer",
    )(x_flat, x_flat, w_all, b_all, conv_w, mh_g, w_out, ln_g, ln_b, rw_hi, rw_lo, rb)


def _sc_mesh():
    return plsc.VectorSubcoreMesh(core_axis_name="core", subcore_axis_name="subcore")


def _sc_scatter_pair(x_flat, dest, n_slots):
    n, d = x_flat.shape
    windows = [dest[k].reshape(n // SC_WINDOW, 1, SC_WINDOW) for k in range(TOP_K)]

    @functools.partial(
        pl.kernel, out_type=jax.ShapeDtypeStruct((n_slots, d), x_flat.dtype), mesh=_sc_mesh(),
        scratch_types=[])
    def scatter_kernel(x_hbm, *rest):
        idx_hbm, o_hbm = rest[:TOP_K], rest[TOP_K]

        def body(x_vmem, *idx_vmem):
            for i_vmem in idx_vmem:
                pltpu.sync_copy(x_vmem, o_hbm.at[i_vmem.at[0, 0]])

        pltpu.emit_pipeline(
            body,
            grid=(n // SC_WINDOW,),
            in_specs=[pl.BlockSpec((SC_WINDOW, d), lambda i: (i, 0))]
            + [pl.BlockSpec((1, 1, SC_WINDOW), lambda i: (i, 0, 0))] * TOP_K,
            out_specs=[],
            core_axis_name=("core", "subcore"),
            dimension_semantics=(pltpu.PARALLEL,),
        )(x_hbm, *idx_hbm)

    return scatter_kernel(x_flat, *windows)


def _expert_kernel(be_ref, nv_ref, br_ref, buf_ref, wg_ref, wu_ref, wd_ref, y_ref, wgu_scr, wd_scr):
    step = pl.program_id(0)
    used = step < nv_ref[0]

    @pl.when(jnp.logical_and(used, jnp.logical_or(step == 0, be_ref[step] != be_ref[jnp.maximum(step - 1, 0)])))
    def _():
        wgu_scr[:, :D_FF] = wg_ref[0, 0].astype(BF16)
        wgu_scr[:, D_FF:] = wu_ref[0, 0].astype(BF16)
        wd_scr[...] = wd_ref[0, 0].astype(BF16)

    @pl.when(used)
    def _():
        rid = lax.broadcasted_iota(jnp.int32, (MOE_BLOCK, 1), 0)
        words = jnp.where(rid < br_ref[step], buf_ref[...], jnp.uint32(0))
        xb = _unpack_bf16(words).astype(BF16)
        h = _dot(xb, wgu_scr[...])
        hdn = jax.nn.silu(h[:, :D_FF]) * h[:, D_FF:]
        y_ref[...] = _pack_bf16(_dot(hdn.astype(BF16), wd_scr[...]))

    @pl.when(jnp.logical_not(used))
    def _():
        y_ref[...] = jnp.zeros_like(y_ref)


def _expert_call(layer, block_expert, n_valid, block_rows, buf, w_gate, w_up, w_down):
    n_slots, dw = buf.shape
    d = 2 * dw
    n_blocks = n_slots // MOE_BLOCK

    def row_map(i, be, nv, br):
        return (jnp.minimum(i, nv[0] - 1), 0)

    def w_map(i, be, nv, br):
        return (layer, be[jnp.minimum(i, nv[0] - 1)], 0, 0)

    return pl.pallas_call(
        _expert_kernel,
        grid_spec=pltpu.PrefetchScalarGridSpec(
            num_scalar_prefetch=3,
            grid=(n_blocks,),
            in_specs=[
                pl.BlockSpec((MOE_BLOCK, dw), row_map),
                pl.BlockSpec((1, 1, d, D_FF), w_map),
                pl.BlockSpec((1, 1, d, D_FF), w_map),
                pl.BlockSpec((1, 1, D_FF, d), w_map),
            ],
            out_specs=pl.BlockSpec((MOE_BLOCK, dw), lambda i, be, nv, br: (i, 0)),
            scratch_shapes=[pltpu.VMEM((d, 2 * D_FF), BF16), pltpu.VMEM((D_FF, d), BF16)],
        ),
        out_shape=jax.ShapeDtypeStruct((n_slots, dw), jnp.uint32),
        compiler_params=pltpu.CompilerParams(
            dimension_semantics=("arbitrary",), vmem_limit_bytes=VMEM_LIMIT),
        name="experts",
    )(block_expert, n_valid, block_rows, buf, w_gate, w_up, w_down)


def _sc_gather(table, idx):
    m = idx.shape[0]
    d = table.shape[1]

    @functools.partial(
        pl.kernel, out_type=jax.ShapeDtypeStruct((m, d), table.dtype), mesh=_sc_mesh(),
        scratch_types=[])
    def gather_kernel(t_hbm, i_hbm, o_hbm):
        def body(i_vmem, o_vmem):
            pltpu.sync_copy(t_hbm.at[i_vmem.at[0, 0]], o_vmem)

        pltpu.emit_pipeline(
            body,
            grid=(m // SC_WINDOW,),
            in_specs=[pl.BlockSpec((1, 1, SC_WINDOW), lambda i: (i, 0, 0))],
            out_specs=[pl.BlockSpec((SC_WINDOW, d), lambda i: (i, 0))],
            core_axis_name=("core", "subcore"),
            dimension_semantics=(pltpu.PARALLEL,),
        )(i_hbm, o_hbm)

    return gather_kernel(table, idx.reshape(m // SC_WINDOW, 1, SC_WINDOW))


def _combine_kernel(x_ref, y0_ref, y1_ref, gate_ref, lng_ref, lnb_ref, o_ref):
    ffn = _unpack_bf16(y0_ref[...]) * gate_ref[:, 0:1] + _unpack_bf16(y1_ref[...]) * gate_ref[:, 1:2]
    o_ref[...] = _layer_norm(ALPHA * x_ref[...] + ffn, lng_ref[...], lnb_ref[...])


def _combine_call(x_flat, dest, gates_t, y_buf, ln_g, ln_b):
    n, d = x_flat.shape
    rows = min(2 * ROW_TILE, n)
    nb = n // rows
    yg = _sc_gather(y_buf, dest.reshape(TOP_K * n))
    return pl.pallas_call(
        _combine_kernel,
        grid=(nb,),
        in_specs=[
            pl.BlockSpec((rows, d), lambda i: (i, 0)),
            pl.BlockSpec((rows, d // 2), lambda i: (i, 0)),
            pl.BlockSpec((rows, d // 2), lambda i: (i + nb, 0)),
            pl.BlockSpec((rows, TOP_K), lambda i: (i, 0)),
            _const_spec(ln_g.shape), _const_spec(ln_b.shape),
        ],
        out_specs=pl.BlockSpec((rows, d), lambda i: (i, 0)),
        out_shape=jax.ShapeDtypeStruct((n, d), F32),
        compiler_params=pltpu.CompilerParams(
            dimension_semantics=("arbitrary",), vmem_limit_bytes=VMEM_LIMIT),
        name="combine",
    )(x_flat, yg, yg, gates_t, ln_g, ln_b)


def _slot_plan(counts, eidx, rank, n_blocks):
    padded = (counts + MOE_BLOCK - 1) // MOE_BLOCK * MOE_BLOCK
    pad_end = jnp.cumsum(padded)
    pad_start = pad_end - padded
    dest = rank
    for e in range(N_EXPERTS):
        dest = dest + jnp.where(eidx == e, pad_start[e], 0)
    block_row0 = jnp.arange(n_blocks, dtype=jnp.int32) * MOE_BLOCK
    block_expert = jnp.minimum(
        jnp.sum(pad_end[None, :] <= block_row0[:, None], axis=1), N_EXPERTS - 1).astype(jnp.int32)
    n_valid = (pad_end[-1:] // MOE_BLOCK).astype(jnp.int32)
    block_rows = jnp.clip(
        (pad_start + counts)[block_expert] - block_row0, 0, MOE_BLOCK).astype(jnp.int32)
    return dest.astype(jnp.int32), block_expert, n_valid, block_rows


def kernel(x, w_in, b_in, conv_w, mh_norm_g, w_out, ln_mix_g, ln_mix_b, router_w, router_b,
           w_gate, w_up, w_down, ln_moe_g, ln_moe_b):
    bsz, seq, d = x.shape
    n = bsz * seq
    n_slots = n * TOP_K + N_EXPERTS * MOE_BLOCK
    n_blocks = n_slots // MOE_BLOCK

    gate_pad = jnp.zeros(w_in.shape[:2] + (LANES - N_HEADS,), w_in.dtype)
    w_all = jnp.concatenate(
        [w_in[..., OFF_Q:OFF_I], w_in[..., OFF_B:P_IN], w_in[..., OFF_I:OFF_F], gate_pad,
         w_in[..., OFF_F:OFF_B], gate_pad], axis=-1).astype(BF16)
    b_all = jnp.concatenate(
        [b_in[..., OFF_Q:OFF_I], b_in[..., OFF_B:P_IN], b_in[..., OFF_I:OFF_F], gate_pad[:, 0, :],
         b_in[..., OFF_F:OFF_B], gate_pad[:, 0, :]], axis=-1).astype(F32)[:, None, :]
    w_out_b = w_out.astype(BF16)
    rw_t = router_w.T.astype(F32)
    rw_hi, rw_lo = _split_bf16(rw_t)
    rb = router_b.astype(F32)[:, None]

    xf = x.reshape(n, d)
    for l in range(DEPTH):
        x1, x1p, eidx, gates, rank, cnt = _mixer_call(
            xf, seq, w_all[l], b_all[l], conv_w[l], mh_norm_g[l][None, :], w_out_b[l],
            ln_mix_g[l][None, :], ln_mix_b[l][None, :], rw_hi, rw_lo, rb)
        dest, block_expert, n_valid, block_rows = _slot_plan(cnt[:, 0], eidx, rank, n_blocks)
        buf = _sc_scatter_pair(x1p, dest, n_slots)
        y_buf = _expert_call(l, block_expert, n_valid, block_rows, buf, w_gate, w_up, w_down)
        xf = _combine_call(x1, dest, gates.T, y_buf, ln_moe_g[l][None, :], ln_moe_b[l][None, :])
    return xf.reshape(bsz, seq, d)
```

```python
import functools

import jax
import jax.numpy as jnp
from jax import lax
from jax.experimental import pallas as pl
from jax.experimental.pallas import tpu as pltpu
from jax.experimental.pallas import tpu_sc as plsc

D_MODEL = 1024
DEPTH = 4
D_MLSTM = D_MODEL // 2
D_CONV = D_MODEL - D_MLSTM
N_HEADS = 4
HEAD_DIM = D_MLSTM // N_HEADS
CONV_WIDTH = 3
CHUNK = 128
N_EXPERTS = 16
N_GROUPS = 4
EXPERTS_PER_GROUP = N_EXPERTS // N_GROUPS
TOP_K = 2
D_FF = D_MODEL // 2
LN_EPS = 1e-5
ALPHA = (2 * DEPTH) ** 0.25

OFF_Q = 0
OFF_K = OFF_Q + D_MLSTM
OFF_V = OFF_K + D_MLSTM
OFF_O = OFF_V + D_MLSTM
OFF_I = OFF_O + D_MLSTM
OFF_F = OFF_I + N_HEADS
OFF_B = OFF_F + N_HEADS
OFF_C = OFF_B + D_CONV
OFF_X = OFF_C + D_CONV
P_IN = OFF_X + D_CONV

LANES = 128
W_M = 4 * D_MLSTM
W_C = 3 * D_CONV
GATE_I = W_M + W_C
GATE_F = GATE_I + LANES
W_ALL = GATE_F + LANES

MIX_ROWS = 512
PIECE_COLS = 256
MOE_BLOCK = 512
ROW_TILE = 256
N_CHAINS = 2
SC_WINDOW = 64
VMEM_LIMIT = 56 * 1024 * 1024

F32 = jnp.float32
BF16 = jnp.bfloat16


def _dot(a, b):
    return jnp.dot(a, b, preferred_element_type=F32)


def _dot_nt(a, b):
    return lax.dot_general(a, b, (((1,), (1,)), ((), ())), preferred_element_type=F32)


def _split_bf16(x):
    hi = x.astype(BF16)
    lo = (x - hi.astype(F32)).astype(BF16)
    return hi, lo


def _pack_bf16(x):
    c = x.shape[1] // 2
    bits = lax.bitcast_convert_type(x.astype(BF16).astype(F32), jnp.uint32)
    return (bits[:, c:] & jnp.uint32(0xFFFF0000)) | (bits[:, :c] >> 16)


def _unpack_bf16(w):
    lo = lax.bitcast_convert_type(w << 16, F32)
    hi = lax.bitcast_convert_type(w & jnp.uint32(0xFFFF0000), F32)
    return jnp.concatenate([lo, hi], axis=1)


def _layer_norm(z, g, b):
    mu = jnp.mean(z, axis=-1, keepdims=True)
    d = z - mu
    var = jnp.mean(d * d, axis=-1, keepdims=True)
    return d * lax.rsqrt(var + LN_EPS) * g + b


def _rows(x, c):
    return x[c * CHUNK:(c + 1) * CHUNK]


def _stack(parts):
    return parts[0] if len(parts) == 1 else jnp.concatenate(parts, axis=0)


def _mlstm_tile(p_ref, y_ref, state_ref, m_ref, mhg_ref, fresh, ts, fill):
    L = CHUNK
    nch = ts // L
    row_i = lax.broadcasted_iota(jnp.int32, (L, L), 0)
    col_i = lax.broadcasted_iota(jnp.int32, (L, L), 1)
    tri = (col_i <= row_i).astype(BF16)
    causal = _stack([col_i <= row_i] * nch)
    ones = jnp.ones((ts, LANES), BF16)

    def per_chunk(vals):
        return _stack([jnp.broadcast_to(vals[c], (L, LANES)) for c in range(nch)])

    ig = p_ref[:, pl.ds(GATE_I, LANES)]
    fg = p_ref[:, pl.ds(GATE_F, LANES)]
    logf = jnp.minimum(fg, 0.0) - jnp.log1p(jnp.exp(-jnp.abs(fg)))
    lf_hi, lf_lo = _split_bf16(logf)
    a = _stack([_dot(tri, _rows(lf_hi, c)) + _dot(tri, _rows(lf_lo, c)) for c in range(nch)])
    b_t = [(_rows(ig, c) - _rows(a, c)).T for c in range(nch)]

    for h in range(N_HEADS):
        c0 = h * HEAD_DIM
        q = p_ref[:, pl.ds(OFF_Q + c0, HEAD_DIM)]
        k = p_ref[:, pl.ds(OFF_K + c0, HEAD_DIM)] * (HEAD_DIM ** -0.5)
        v = p_ref[:, pl.ds(OFF_V + c0, HEAD_DIM)]
        o = p_ref[:, pl.ds(OFF_O + c0, HEAD_DIM)]

        a_b = jnp.broadcast_to(a[:, h:h + 1], (ts, LANES))
        ig_b = jnp.broadcast_to(ig[:, h:h + 1], (ts, LANES))
        g = [a_b[(c + 1) * L - 1:(c + 1) * L, :] for c in range(nch)]
        w_end = per_chunk(g) - a_b + ig_b
        m_loc = [jnp.max(_rows(w_end, c), axis=0, keepdims=True) for c in range(nch)]
        m = [jnp.where(fresh, 0.0, m_ref[h:h + 1, :])]
        for c in range(nch):
            m.append(jnp.maximum(g[c] + m[c], m_loc[c]))
        m_ref[h:h + 1, :] = m[nch]
        log_inter = a_b + per_chunk(m)
        e_end = jnp.exp(w_end - per_chunk(m_loc))

        b_rows = _stack([jnp.broadcast_to(b_t[c][h:h + 1, :], (L, L)) for c in range(nch)])
        log_d = jnp.where(causal, a_b + b_rows, -jnp.inf)
        m_out = jnp.maximum(log_inter, jnp.max(log_d, axis=-1, keepdims=True))
        dmat = jnp.exp(log_d - m_out)
        fill()
        qb = q.astype(BF16)
        kb = k.astype(BF16)
        v_aug = jnp.concatenate([v.astype(BF16), ones], axis=1)
        s_qk = (_stack([_dot_nt(_rows(qb, c), _rows(kb, c)) for c in range(nch)]) * dmat).astype(BF16)
        intra = _stack([_dot(_rows(s_qk, c), _rows(v_aug, c)) for c in range(nch)])
        k_e = k * e_end
        s_loc = [_dot(_rows(k_e, c).T.astype(BF16), _rows(v_aug, c)) for c in range(nch)]
        state = [jnp.where(fresh, 0.0, state_ref[h])]
        for c in range(nch):
            keep = jnp.exp(g[c] + m[c] - m[c + 1])
            take = jnp.exp(m_loc[c] - m[c + 1])
            state.append(jnp.concatenate([keep, keep], axis=1) * state[c]
                         + jnp.concatenate([take, take], axis=1) * s_loc[c])
        state_ref[h] = state[nch]
        inter = _stack([_dot(_rows(qb, c), state[c].astype(BF16)) for c in range(nch)])
        e_inter = jnp.exp(log_inter - m_out)
        num = intra[:, :HEAD_DIM] + e_inter * inter[:, :HEAD_DIM]
        den = intra[:, HEAD_DIM:] + e_inter * inter[:, HEAD_DIM:]
        hh = num / jnp.maximum(jnp.abs(den), jnp.exp(-m_out))

        mu = jnp.mean(hh, axis=-1, keepdims=True)
        dh = hh - mu
        var = jnp.mean(dh * dh, axis=-1, keepdims=True)
        hn = dh * lax.rsqrt(var + LN_EPS) * mhg_ref[:, pl.ds(c0, HEAD_DIM)]
        y_ref[:, pl.ds(c0, HEAD_DIM)] = (jax.nn.sigmoid(o) * hn).astype(y_ref.dtype)
        fill()


def _route(logits_t, ts):
    lg = [logits_t[e:e + 1, :] for e in range(N_EXPERTS)]
    mx = functools.reduce(jnp.maximum, lg)
    ex = [jnp.exp(v - mx) for v in lg]
    tot = functools.reduce(lambda a, b: a + b, ex)
    p = [v / tot for v in ex]
    scores = []
    for g in range(N_GROUPS):
        p0, p1, p2, p3 = p[g * EXPERTS_PER_GROUP:(g + 1) * EXPERTS_PER_GROUP]
        hi01, lo01 = jnp.maximum(p0, p1), jnp.minimum(p0, p1)
        hi23, lo23 = jnp.maximum(p2, p3), jnp.minimum(p2, p3)
        top1 = jnp.maximum(hi01, hi23)
        top2 = jnp.maximum(jnp.minimum(hi01, hi23), jnp.maximum(lo01, lo23))
        scores.append(top1 + top2)
    grp = jnp.zeros((1, ts), jnp.int32)
    best = scores[0]
    for g in range(1, N_GROUPS):
        upd = scores[g] > best
        best = jnp.where(upd, scores[g], best)
        grp = jnp.where(upd, g, grp)
    sel = []
    for j in range(EXPERTS_PER_GROUP):
        v = p[j]
        for g in range(1, N_GROUPS):
            v = jnp.where(grp == g, p[g * EXPERTS_PER_GROUP + j], v)
        sel.append(v)
    i1 = jnp.zeros((1, ts), jnp.int32)
    v1 = sel[0]
    for j in range(1, EXPERTS_PER_GROUP):
        upd = sel[j] > v1
        v1 = jnp.where(upd, sel[j], v1)
        i1 = jnp.where(upd, j, i1)
    i2 = jnp.zeros((1, ts), jnp.int32)
    v2 = jnp.full((1, ts), -jnp.inf, F32)
    for j in range(EXPERTS_PER_GROUP):
        cand = jnp.where(i1 == j, -jnp.inf, sel[j])
        upd = cand > v2
        v2 = jnp.where(upd, cand, v2)
        i2 = jnp.where(upd, j, i2)
    e0 = grp * EXPERTS_PER_GROUP + i1
    e1 = grp * EXPERTS_PER_GROUP + i2
    s12 = v1 + v2
    return e0, e1, v1 / s12, v2 / s12


def _in_proj_pieces(x, w_ref, b_ref, p_ref):
    xb = x.astype(BF16)

    def piece(j):
        cols = pl.ds(j * PIECE_COLS, PIECE_COLS)
        p_ref[:, cols] = _dot(xb, w_ref[:, cols]) + b_ref[:, cols]

    return [functools.partial(piece, j) for j in range(W_ALL // PIECE_COLS)]


def _in_proj(x, w_ref, b_ref, p_ref):
    for piece in _in_proj_pieces(x, w_ref, b_ref, p_ref):
        piece()


class _Filler:
    def __init__(self, thunks):
        self._thunks = list(thunks)

    def __call__(self, count=1):
        for _ in range(count):
            if self._thunks:
                self._thunks.pop(0)()

    def drain(self):
        self(len(self._thunks))


def _mix_tile(p_ref, x, r0, fresh, first_tile, refs, ts, fill):
    (cw_ref, mhg_ref, wo_ref, lng_ref, lnb_ref, rwh_ref, rwl_ref, rb_ref,
     xo_ref, xp_ref, eidx_ref, gate_ref, rank_ref, cnt_ref,
     y_ref, state_ref, m_ref, ucarry_ref, cnt_scr) = refs
    _mlstm_tile(p_ref, y_ref, state_ref, m_ref, mhg_ref, fresh, ts, fill)

    gate_b = p_ref[:, pl.ds(W_M, D_CONV)]
    u = p_ref[:, pl.ds(W_M + D_CONV, D_CONV)] * p_ref[:, pl.ds(W_M + 2 * D_CONV, D_CONV)]
    rid = lax.broadcasted_iota(jnp.int32, (ts, D_CONV), 0)
    prev1 = jnp.where(fresh, 0.0, ucarry_ref[7:8, :])
    prev2 = jnp.where(fresh, 0.0, ucarry_ref[6:7, :])
    u1 = jnp.where(rid == 0, prev1, pltpu.roll(u, 1, 0))
    u2 = jnp.where(rid == 0, prev2, jnp.where(rid == 1, prev1, pltpu.roll(u, 2, 0)))
    conv = cw_ref[0:1, :] * u2 + cw_ref[1:2, :] * u1 + cw_ref[2:3, :] * u
    y_ref[:, pl.ds(D_MLSTM, D_CONV)] = (gate_b * conv).astype(y_ref.dtype)
    ucarry_ref[...] = u[ts - 8:ts, :]
    fill()

    mix = _dot(y_ref[...], wo_ref[...])
    x1 = _layer_norm(ALPHA * x + mix, lng_ref[...], lnb_ref[...])
    xo_ref[pl.ds(r0, ts), :] = x1
    xp_ref[pl.ds(r0, ts), :] = _pack_bf16(x1)
    fill(2)

    x_hi, x_lo = _split_bf16(x1)
    logits_t = (_dot_nt(rwh_ref[...], x_hi) + _dot_nt(rwl_ref[...], x_hi)
                + _dot_nt(rwh_ref[...], x_lo) + rb_ref[...])
    e0, e1, g0, g1 = _route(logits_t, ts)
    eidx_ref[0:1, pl.ds(r0, ts)] = e0
    eidx_ref[1:2, pl.ds(r0, ts)] = e1
    gate_ref[0:1, pl.ds(r0, ts)] = g0
    gate_ref[1:2, pl.ds(r0, ts)] = g1
    fill(2)

    eid = lax.broadcasted_iota(jnp.int32, (N_EXPERTS, ts), 0)
    hit0 = eid == e0
    hit1 = eid == e1
    onehot = hit0.astype(F32) + hit1.astype(F32)
    t_r = lax.broadcasted_iota(jnp.int32, (ts, ts), 0)
    t_c = lax.broadcasted_iota(jnp.int32, (ts, ts), 1)
    upper = (t_r <= t_c).astype(BF16)
    csum = _dot(onehot.astype(BF16), upper)
    seen = jnp.where(first_tile, 0.0, cnt_scr[...])
    before = csum - onehot + seen[:, 0:1]
    rank_ref[0:1, pl.ds(r0, ts)] = jnp.sum(jnp.where(hit0, before, 0.0), axis=0, keepdims=True).astype(jnp.int32)
    rank_ref[1:2, pl.ds(r0, ts)] = jnp.sum(jnp.where(hit1, before, 0.0), axis=0, keepdims=True).astype(jnp.int32)
    seen = seen + csum[:, ts - 1:ts]
    cnt_scr[...] = seen
    cnt_ref[...] = seen.astype(jnp.int32)


def _mixer_kernel(x_ref, xn_ref, w_ref, b_ref, cw_ref, mhg_ref, wo_ref, lng_ref, lnb_ref,
                  rwh_ref, rwl_ref, rb_ref,
                  xo_ref, xp_ref, eidx_ref, gate_ref, rank_ref, cnt_ref,
                  pa_ref, pb_ref, y_ref, state_ref, m_ref, ucarry_ref, cnt_scr, *, ts, seq):
    step = pl.program_id(0)
    refs = (cw_ref, mhg_ref, wo_ref, lng_ref, lnb_ref, rwh_ref, rwl_ref, rb_ref,
            xo_ref, xp_ref, eidx_ref, gate_ref, rank_ref, cnt_ref,
            y_ref, state_ref, m_ref, ucarry_ref, cnt_scr)

    @pl.when(step == 0)
    def _():
        _in_proj(x_ref[0:ts, :], w_ref, b_ref, pa_ref)

    tile_a = 2 * step
    fill = _Filler(_in_proj_pieces(x_ref[ts:2 * ts, :], w_ref, b_ref, pb_ref))
    _mix_tile(pa_ref, x_ref[0:ts, :], 0, (tile_a * ts) % seq == 0, tile_a == 0, refs, ts, fill)
    fill.drain()
    fill = _Filler(_in_proj_pieces(xn_ref[...], w_ref, b_ref, pa_ref))
    _mix_tile(pb_ref, x_ref[ts:2 * ts, :], ts, ((tile_a + 1) * ts) % seq == 0, False, refs, ts, fill)
    fill.drain()


def _const_spec(shape):
    nd = len(shape)
    return pl.BlockSpec(shape, lambda *_: (0,) * nd, pipeline_mode=pl.Buffered(1))


def _mixer_call(x_flat, row0, n, seq, w_all, b_all, conv_w, mh_g, w_out, ln_g, ln_b, rw_hi, rw_lo, rb):
    d = x_flat.shape[1]
    ts = min(MIX_ROWS, seq // 2)
    n_steps = n // (2 * ts)
    step0 = row0 // (2 * ts)
    tok_spec = pl.BlockSpec((TOP_K, 2 * ts), lambda i: (0, i))
    return pl.pallas_call(
        functools.partial(_mixer_kernel, ts=ts, seq=seq),
        grid=(n_steps,),
        in_specs=[
            pl.BlockSpec((2 * ts, d), lambda i: (step0 + i, 0)),
            pl.BlockSpec((ts, d), lambda i: (2 * (step0 + jnp.minimum(i + 1, n_steps - 1)), 0)),
            _const_spec(w_all.shape), _const_spec(b_all.shape), _const_spec(conv_w.shape),
            _const_spec(mh_g.shape), _const_spec(w_out.shape), _const_spec(ln_g.shape),
            _const_spec(ln_b.shape), _const_spec(rw_hi.shape), _const_spec(rw_lo.shape),
            _const_spec(rb.shape),
        ],
        out_specs=[
            pl.BlockSpec((2 * ts, d), lambda i: (i, 0)),
            pl.BlockSpec((2 * ts, d // 2), lambda i: (i, 0)),
            tok_spec, tok_spec, tok_spec,
            pl.BlockSpec((N_EXPERTS, LANES), lambda i: (0, 0)),
        ],
        out_shape=[
            jax.ShapeDtypeStruct((n, d), F32),
            jax.ShapeDtypeStruct((n, d // 2), jnp.uint32),
            jax.ShapeDtypeStruct((TOP_K, n), jnp.int32),
            jax.ShapeDtypeStruct((TOP_K, n), F32),
            jax.ShapeDtypeStruct((TOP_K, n), jnp.int32),
            jax.ShapeDtypeStruct((N_EXPERTS, LANES), jnp.int32),
        ],
        scratch_shapes=[
            pltpu.VMEM((ts, W_ALL), F32),
            pltpu.VMEM((ts, W_ALL), F32),
            pltpu.VMEM((ts, D_MODEL), BF16),
            pltpu.VMEM((N_HEADS, HEAD_DIM, 2 * HEAD_DIM), F32),
            pltpu.VMEM((8, LANES), F32),
            pltpu.VMEM((8, D_CONV), F32),
            pltpu.VMEM((N_EXPERTS, LANES), F32),
        ],
        compiler_params=pltpu.CompilerParams(
            dimension_semantics=("arbitrary",), vmem_limit_bytes=VMEM_LIMIT),
        name="mixer",
    )(x_flat, x_flat, w_all, b_all, conv_w, mh_g, w_out, ln_g, ln_b, rw_hi, rw_lo, rb)


def _sc_mesh():
    return plsc.VectorSubcoreMesh(core_axis_name="core", subcore_axis_name="subcore")


def _sc_scatter_pair(x_flat, dest, n_slots):
    n, d = x_flat.shape
    windows = [dest[k].reshape(n // SC_WINDOW, 1, SC_WINDOW) for k in range(TOP_K)]

    @functools.partial(
        pl.kernel, out_type=jax.ShapeDtypeStruct((n_slots, d), x_flat.dtype), mesh=_sc_mesh(),
        scratch_types=[])
    def scatter_kernel(x_hbm, *rest):
        idx_hbm, o_hbm = rest[:TOP_K], rest[TOP_K]

        def body(x_vmem, *idx_vmem):
            for i_vmem in idx_vmem:
                pltpu.sync_copy(x_vmem, o_hbm.at[i_vmem.at[0, 0]])

        pltpu.emit_pipeline(
            body,
            grid=(n // SC_WINDOW,),
            in_specs=[pl.BlockSpec((SC_WINDOW, d), lambda i: (i, 0))]
            + [pl.BlockSpec((1, 1, SC_WINDOW), lambda i: (i, 0, 0))] * TOP_K,
            out_specs=[],
            core_axis_name=("core", "subcore"),
            dimension_semantics=(pltpu.PARALLEL,),
        )(x_hbm, *idx_hbm)

    return scatter_kernel(x_flat, *windows)


def _expert_kernel(be_ref, nv_ref, br_ref, buf_ref, wg_ref, wu_ref, wd_ref, y_ref, wgu_scr, wd_scr):
    step = pl.program_id(0)
    used = step < nv_ref[0]

    @pl.when(jnp.logical_and(used, jnp.logical_or(step == 0, be_ref[step] != be_ref[jnp.maximum(step - 1, 0)])))
    def _():
        wgu_scr[:, :D_FF] = wg_ref[0, 0].astype(BF16)
        wgu_scr[:, D_FF:] = wu_ref[0, 0].astype(BF16)
        wd_scr[...] = wd_ref[0, 0].astype(BF16)

    @pl.when(used)
    def _():
        rid = lax.broadcasted_iota(jnp.int32, (MOE_BLOCK, 1), 0)
        words = jnp.where(rid < br_ref[step], buf_ref[...], jnp.uint32(0))
        xb = _unpack_bf16(words).astype(BF16)
        h = _dot(xb, wgu_scr[...])
        hdn = jax.nn.silu(h[:, :D_FF]) * h[:, D_FF:]
        y_ref[...] = _pack_bf16(_dot(hdn.astype(BF16), wd_scr[...]))

    @pl.when(jnp.logical_not(used))
    def _():
        y_ref[...] = jnp.zeros_like(y_ref)


def _expert_call(layer, block_expert, n_valid, block_rows, buf, w_gate, w_up, w_down):
    n_slots, dw = buf.shape
    d = 2 * dw
    n_blocks = n_slots // MOE_BLOCK

    def row_map(i, be, nv, br):
        return (jnp.minimum(i, nv[0] - 1), 0)

    def w_map(i, be, nv, br):
        return (layer, be[jnp.minimum(i, nv[0] - 1)], 0, 0)

    return pl.pallas_call(
        _expert_kernel,
        grid_spec=pltpu.PrefetchScalarGridSpec(
            num_scalar_prefetch=3,
            grid=(n_blocks,),
            in_specs=[
                pl.BlockSpec((MOE_BLOCK, dw), row_map),
                pl.BlockSpec((1, 1, d, D_FF), w_map),
                pl.BlockSpec((1, 1, d, D_FF), w_map),
                pl.BlockSpec((1, 1, D_FF, d), w_map),
            ],
            out_specs=pl.BlockSpec((MOE_BLOCK, dw), lambda i, be, nv, br: (i, 0)),
            scratch_shapes=[pltpu.VMEM((d, 2 * D_FF), BF16), pltpu.VMEM((D_FF, d), BF16)],
        ),
        out_shape=jax.ShapeDtypeStruct((n_slots, dw), jnp.uint32),
        compiler_params=pltpu.CompilerParams(
            dimension_semantics=("arbitrary",), vmem_limit_bytes=VMEM_LIMIT),
        name="experts",
    )(block_expert, n_valid, block_rows, buf, w_gate, w_up, w_down)


def _sc_gather(table, idx):
    m = idx.shape[0]
    d = table.shape[1]

    @functools.partial(
        pl.kernel, out_type=jax.ShapeDtypeStruct((m, d), table.dtype), mesh=_sc_mesh(),
        scratch_types=[])
    def gather_kernel(t_hbm, i_hbm, o_hbm):
        def body(i_vmem, o_vmem):
            pltpu.sync_copy(t_hbm.at[i_vmem.at[0, 0]], o_vmem)

        pltpu.emit_pipeline(
            body,
            grid=(m // SC_WINDOW,),
            in_specs=[pl.BlockSpec((1, 1, SC_WINDOW), lambda i: (i, 0, 0))],
            out_specs=[pl.BlockSpec((SC_WINDOW, d), lambda i: (i, 0))],
            core_axis_name=("core", "subcore"),
            dimension_semantics=(pltpu.PARALLEL,),
        )(i_hbm, o_hbm)

    return gather_kernel(table, idx.reshape(m // SC_WINDOW, 1, SC_WINDOW))


def _combine_kernel(x_ref, y0_ref, y1_ref, gate_ref, lng_ref, lnb_ref, *rest):
    o_ref = rest[-1]
    ffn = _unpack_bf16(y0_ref[...]) * gate_ref[:, 0:1] + _unpack_bf16(y1_ref[...]) * gate_ref[:, 1:2]
    o_ref[...] = _layer_norm(ALPHA * x_ref[...] + ffn, lng_ref[...], lnb_ref[...])


def _combine_call(x_flat, dest, gates_t, y_buf, ln_g, ln_b, out_rows=None, out_row0=0, out_prev=None):
    n, d = x_flat.shape
    rows = min(2 * ROW_TILE, n)
    nb = n // rows
    b0 = out_row0 // rows
    yg = _sc_gather(y_buf, dest.reshape(TOP_K * n))
    operands = [x_flat, yg, yg, gates_t, ln_g, ln_b]
    in_specs = [
        pl.BlockSpec((rows, d), lambda i: (i, 0)),
        pl.BlockSpec((rows, d // 2), lambda i: (i, 0)),
        pl.BlockSpec((rows, d // 2), lambda i: (i + nb, 0)),
        pl.BlockSpec((rows, TOP_K), lambda i: (i, 0)),
        _const_spec(ln_g.shape), _const_spec(ln_b.shape),
    ]
    aliases = {}
    if out_prev is not None:
        operands.append(out_prev)
        in_specs.append(pl.BlockSpec(memory_space=pl.ANY))
        aliases = {6: 0}
    return pl.pallas_call(
        _combine_kernel,
        grid=(nb,),
        in_specs=in_specs,
        out_specs=pl.BlockSpec((rows, d), lambda i: (b0 + i, 0)),
        out_shape=jax.ShapeDtypeStruct((out_rows or n, d), F32),
        input_output_aliases=aliases,
        compiler_params=pltpu.CompilerParams(
            dimension_semantics=("arbitrary",), vmem_limit_bytes=VMEM_LIMIT),
        name="combine",
    )(*operands)


def _slot_plan(counts, eidx, rank, n_blocks):
    padded = (counts + MOE_BLOCK - 1) // MOE_BLOCK * MOE_BLOCK
    pad_end = jnp.cumsum(padded)
    pad_start = pad_end - padded
    dest = rank
    for e in range(N_EXPERTS):
        dest = dest + jnp.where(eidx == e, pad_start[e], 0)
    block_row0 = jnp.arange(n_blocks, dtype=jnp.int32) * MOE_BLOCK
    block_expert = jnp.minimum(
        jnp.sum(pad_end[None, :] <= block_row0[:, None], axis=1), N_EXPERTS - 1).astype(jnp.int32)
    n_valid = (pad_end[-1:] // MOE_BLOCK).astype(jnp.int32)
    block_rows = jnp.clip(
        (pad_start + counts)[block_expert] - block_row0, 0, MOE_BLOCK).astype(jnp.int32)
    return dest.astype(jnp.int32), block_expert, n_valid, block_rows


def kernel(x, w_in, b_in, conv_w, mh_norm_g, w_out, ln_mix_g, ln_mix_b, router_w, router_b,
           w_gate, w_up, w_down, ln_moe_g, ln_moe_b):
    bsz, seq, d = x.shape
    n = bsz * seq

    gate_pad = jnp.zeros(w_in.shape[:2] + (LANES - N_HEADS,), w_in.dtype)
    w_all = jnp.concatenate(
        [w_in[..., OFF_Q:OFF_I], w_in[..., OFF_B:P_IN], w_in[..., OFF_I:OFF_F], gate_pad,
         w_in[..., OFF_F:OFF_B], gate_pad], axis=-1).astype(BF16)
    b_all = jnp.concatenate(
        [b_in[..., OFF_Q:OFF_I], b_in[..., OFF_B:P_IN], b_in[..., OFF_I:OFF_F], gate_pad[:, 0, :],
         b_in[..., OFF_F:OFF_B], gate_pad[:, 0, :]], axis=-1).astype(F32)[:, None, :]
    w_out_b = w_out.astype(BF16)
    rw_t = router_w.T.astype(F32)
    rw_hi, rw_lo = _split_bf16(rw_t)
    rb = router_b.astype(F32)[:, None]

    n_chain = N_CHAINS if bsz % N_CHAINS == 0 else 1
    nc = n // n_chain
    n_slots = nc * TOP_K + N_EXPERTS * MOE_BLOCK
    n_blocks = n_slots // MOE_BLOCK
    x_full = x.reshape(n, d)
    acts = [(x_full, c * nc) for c in range(n_chain)]
    out = None
    for l in range(DEPTH):
        last = l == DEPTH - 1
        for c in range(n_chain):
            xin, row0 = acts[c]
            x1, x1p, eidx, gates, rank, cnt = _mixer_call(
                xin, row0, nc, seq, w_all[l], b_all[l], conv_w[l], mh_norm_g[l][None, :], w_out_b[l],
                ln_mix_g[l][None, :], ln_mix_b[l][None, :], rw_hi, rw_lo, rb)
            dest, block_expert, n_valid, block_rows = _slot_plan(cnt[:, 0], eidx, rank, n_blocks)
            buf = _sc_scatter_pair(x1p, dest, n_slots)
            y_buf = _expert_call(l, block_expert, n_valid, block_rows, buf, w_gate, w_up, w_down)
            if last:
                out = _combine_call(x1, dest, gates.T, y_buf, ln_moe_g[l][None, :], ln_moe_b[l][None, :],
                                    out_rows=n, out_row0=c * nc, out_prev=out)
            else:
                acts[c] = (_combine_call(x1, dest, gates.T, y_buf, ln_moe_g[l][None, :],
                                         ln_moe_b[l][None, :]), 0)
    return out.reshape(bsz, seq, d)
```

```python
import functools

import jax
import jax.numpy as jnp
from jax import lax
from jax.experimental import pallas as pl
from jax.experimental.pallas import tpu as pltpu
from jax.experimental.pallas import tpu_sc as plsc

D_MODEL = 1024
DEPTH = 4
D_MLSTM = D_MODEL // 2
D_CONV = D_MODEL - D_MLSTM
N_HEADS = 4
HEAD_DIM = D_MLSTM // N_HEADS
CONV_WIDTH = 3
CHUNK = 128
N_EXPERTS = 16
N_GROUPS = 4
EXPERTS_PER_GROUP = N_EXPERTS // N_GROUPS
TOP_K = 2
D_FF = D_MODEL // 2
LN_EPS = 1e-5
ALPHA = (2 * DEPTH) ** 0.25

OFF_Q = 0
OFF_K = OFF_Q + D_MLSTM
OFF_V = OFF_K + D_MLSTM
OFF_O = OFF_V + D_MLSTM
OFF_I = OFF_O + D_MLSTM
OFF_F = OFF_I + N_HEADS
OFF_B = OFF_F + N_HEADS
OFF_C = OFF_B + D_CONV
OFF_X = OFF_C + D_CONV
P_IN = OFF_X + D_CONV

LANES = 128
W_M = 4 * D_MLSTM
W_C = 3 * D_CONV
GATE_I = W_M + W_C
GATE_F = GATE_I + LANES
W_ALL = GATE_F + LANES

MIX_ROWS = 512
PIECE_COLS = 256
MOE_BLOCK = 1024
ROW_TILE = 256
N_CHAINS = 1
SC_WINDOW = 64
VMEM_LIMIT = 56 * 1024 * 1024

F32 = jnp.float32
BF16 = jnp.bfloat16


def _dot(a, b):
    return jnp.dot(a, b, preferred_element_type=F32)


def _dot_nt(a, b):
    return lax.dot_general(a, b, (((1,), (1,)), ((), ())), preferred_element_type=F32)


def _split_bf16(x):
    hi = x.astype(BF16)
    lo = (x - hi.astype(F32)).astype(BF16)
    return hi, lo


def _pack_bf16(x):
    c = x.shape[1] // 2
    bits = lax.bitcast_convert_type(x.astype(BF16).astype(F32), jnp.uint32)
    return (bits[:, c:] & jnp.uint32(0xFFFF0000)) | (bits[:, :c] >> 16)


def _unpack_bf16(w):
    lo = lax.bitcast_convert_type(w << 16, F32)
    hi = lax.bitcast_convert_type(w & jnp.uint32(0xFFFF0000), F32)
    return jnp.concatenate([lo, hi], axis=1)


def _layer_norm(z, g, b):
    mu = jnp.mean(z, axis=-1, keepdims=True)
    d = z - mu
    var = jnp.mean(d * d, axis=-1, keepdims=True)
    return d * lax.rsqrt(var + LN_EPS) * g + b


def _rows(x, c):
    return x[c * CHUNK:(c + 1) * CHUNK]


def _stack(parts):
    return parts[0] if len(parts) == 1 else jnp.concatenate(parts, axis=0)


def _mlstm_tile(p_ref, y_ref, state_ref, m_ref, mhg_ref, fresh, ts, fill):
    L = CHUNK
    nch = ts // L
    row_i = lax.broadcasted_iota(jnp.int32, (L, L), 0)
    col_i = lax.broadcasted_iota(jnp.int32, (L, L), 1)
    tri = (col_i <= row_i).astype(BF16)
    causal = _stack([col_i <= row_i] * nch)
    ones = jnp.ones((ts, LANES), BF16)

    def per_chunk(vals):
        return _stack([jnp.broadcast_to(vals[c], (L, LANES)) for c in range(nch)])

    ig = p_ref[:, pl.ds(GATE_I, LANES)]
    fg = p_ref[:, pl.ds(GATE_F, LANES)]
    logf = jnp.minimum(fg, 0.0) - jnp.log1p(jnp.exp(-jnp.abs(fg)))
    lf_hi, lf_lo = _split_bf16(logf)
    a = _stack([_dot(tri, _rows(lf_hi, c)) + _dot(tri, _rows(lf_lo, c)) for c in range(nch)])
    b_t = [(_rows(ig, c) - _rows(a, c)).T for c in range(nch)]
    fill()

    for h in range(N_HEADS):
        c0 = h * HEAD_DIM
        q = p_ref[:, pl.ds(OFF_Q + c0, HEAD_DIM)]
        k = p_ref[:, pl.ds(OFF_K + c0, HEAD_DIM)] * (HEAD_DIM ** -0.5)
        v = p_ref[:, pl.ds(OFF_V + c0, HEAD_DIM)]
        o = p_ref[:, pl.ds(OFF_O + c0, HEAD_DIM)]

        a_b = jnp.broadcast_to(a[:, h:h + 1], (ts, LANES))
        ig_b = jnp.broadcast_to(ig[:, h:h + 1], (ts, LANES))
        g = [a_b[(c + 1) * L - 1:(c + 1) * L, :] for c in range(nch)]
        w_end = per_chunk(g) - a_b + ig_b
        m_loc = [jnp.max(_rows(w_end, c), axis=0, keepdims=True) for c in range(nch)]
        m = [jnp.where(fresh, 0.0, m_ref[h:h + 1, :])]
        for c in range(nch):
            m.append(jnp.maximum(g[c] + m[c], m_loc[c]))
        m_ref[h:h + 1, :] = m[nch]
        log_inter = a_b + per_chunk(m)
        e_end = jnp.exp(w_end - per_chunk(m_loc))

        b_rows = _stack([jnp.broadcast_to(b_t[c][h:h + 1, :], (L, L)) for c in range(nch)])
        log_d = jnp.where(causal, a_b + b_rows, -jnp.inf)
        m_out = jnp.maximum(log_inter, jnp.max(log_d, axis=-1, keepdims=True))
        dmat = jnp.exp(log_d - m_out)
        if h < 2:
            fill()
        qb = q.astype(BF16)
        kb = k.astype(BF16)
        v_aug = jnp.concatenate([v.astype(BF16), ones], axis=1)
        s_qk = (_stack([_dot_nt(_rows(qb, c), _rows(kb, c)) for c in range(nch)]) * dmat).astype(BF16)
        intra = _stack([_dot(_rows(s_qk, c), _rows(v_aug, c)) for c in range(nch)])
        k_e = k * e_end
        s_loc = [_dot(_rows(k_e, c).T.astype(BF16), _rows(v_aug, c)) for c in range(nch)]
        state = [jnp.where(fresh, 0.0, state_ref[h])]
        for c in range(nch):
            keep = jnp.exp(g[c] + m[c] - m[c + 1])
            take = jnp.exp(m_loc[c] - m[c + 1])
            state.append(jnp.concatenate([keep, keep], axis=1) * state[c]
                         + jnp.concatenate([take, take], axis=1) * s_loc[c])
        state_ref[h] = state[nch]
        inter = _stack([_dot(_rows(qb, c), state[c].astype(BF16)) for c in range(nch)])
        e_inter = jnp.exp(log_inter - m_out)
        num = intra[:, :HEAD_DIM] + e_inter * inter[:, :HEAD_DIM]
        den = intra[:, HEAD_DIM:] + e_inter * inter[:, HEAD_DIM:]
        hh = num / jnp.maximum(jnp.abs(den), jnp.exp(-m_out))

        mu = jnp.mean(hh, axis=-1, keepdims=True)
        dh = hh - mu
        var = jnp.mean(dh * dh, axis=-1, keepdims=True)
        hn = dh * lax.rsqrt(var + LN_EPS) * mhg_ref[:, pl.ds(c0, HEAD_DIM)]
        y_ref[:, pl.ds(c0, HEAD_DIM)] = (jax.nn.sigmoid(o) * hn).astype(y_ref.dtype)
        fill()


def _route(logits_t, ts):
    lg = [logits_t[e:e + 1, :] for e in range(N_EXPERTS)]
    mx = functools.reduce(jnp.maximum, lg)
    ex = [jnp.exp(v - mx) for v in lg]
    tot = functools.reduce(lambda a, b: a + b, ex)
    p = [v / tot for v in ex]
    scores = []
    for g in range(N_GROUPS):
        p0, p1, p2, p3 = p[g * EXPERTS_PER_GROUP:(g + 1) * EXPERTS_PER_GROUP]
        hi01, lo01 = jnp.maximum(p0, p1), jnp.minimum(p0, p1)
        hi23, lo23 = jnp.maximum(p2, p3), jnp.minimum(p2, p3)
        top1 = jnp.maximum(hi01, hi23)
        top2 = jnp.maximum(jnp.minimum(hi01, hi23), jnp.maximum(lo01, lo23))
        scores.append(top1 + top2)
    grp = jnp.zeros((1, ts), jnp.int32)
    best = scores[0]
    for g in range(1, N_GROUPS):
        upd = scores[g] > best
        best = jnp.where(upd, scores[g], best)
        grp = jnp.where(upd, g, grp)
    sel = []
    for j in range(EXPERTS_PER_GROUP):
        v = p[j]
        for g in range(1, N_GROUPS):
            v = jnp.where(grp == g, p[g * EXPERTS_PER_GROUP + j], v)
        sel.append(v)
    i1 = jnp.zeros((1, ts), jnp.int32)
    v1 = sel[0]
    for j in range(1, EXPERTS_PER_GROUP):
        upd = sel[j] > v1
        v1 = jnp.where(upd, sel[j], v1)
        i1 = jnp.where(upd, j, i1)
    i2 = jnp.zeros((1, ts), jnp.int32)
    v2 = jnp.full((1, ts), -jnp.inf, F32)
    for j in range(EXPERTS_PER_GROUP):
        cand = jnp.where(i1 == j, -jnp.inf, sel[j])
        upd = cand > v2
        v2 = jnp.where(upd, cand, v2)
        i2 = jnp.where(upd, j, i2)
    e0 = grp * EXPERTS_PER_GROUP + i1
    e1 = grp * EXPERTS_PER_GROUP + i2
    s12 = v1 + v2
    return e0, e1, v1 / s12, v2 / s12


def _in_proj_pieces(x, w_ref, b_ref, p_ref):
    xb = x.astype(BF16)

    def piece(j):
        cols = pl.ds(j * PIECE_COLS, PIECE_COLS)
        p_ref[:, cols] = _dot(xb, w_ref[:, cols]) + b_ref[:, cols]

    return [functools.partial(piece, j) for j in range(W_ALL // PIECE_COLS)]


def _in_proj(x, w_ref, b_ref, p_ref):
    for piece in _in_proj_pieces(x, w_ref, b_ref, p_ref):
        piece()


class _Filler:
    def __init__(self, thunks):
        self._thunks = list(thunks)

    def __call__(self, count=1):
        for _ in range(count):
            if self._thunks:
                self._thunks.pop(0)()

    def drain(self):
        self(len(self._thunks))


def _mix_tile(p_ref, x, r0, fresh, first_tile, refs, ts, fill):
    (cw_ref, mhg_ref, wo_ref, lng_ref, lnb_ref, rw_ref, rb_ref,
     xo_ref, xp_ref, eidx_ref, gate_ref, rank_ref, cnt_ref,
     y_ref, state_ref, m_ref, ucarry_ref, cnt_scr) = refs
    _mlstm_tile(p_ref, y_ref, state_ref, m_ref, mhg_ref, fresh, ts, fill)

    gate_b = p_ref[:, pl.ds(W_M, D_CONV)]
    u = p_ref[:, pl.ds(W_M + D_CONV, D_CONV)] * p_ref[:, pl.ds(W_M + 2 * D_CONV, D_CONV)]
    rid = lax.broadcasted_iota(jnp.int32, (ts, D_CONV), 0)
    prev1 = jnp.where(fresh, 0.0, ucarry_ref[7:8, :])
    prev2 = jnp.where(fresh, 0.0, ucarry_ref[6:7, :])
    u1 = jnp.where(rid == 0, prev1, pltpu.roll(u, 1, 0))
    u2 = jnp.where(rid == 0, prev2, jnp.where(rid == 1, prev1, pltpu.roll(u, 2, 0)))
    conv = cw_ref[0:1, :] * u2 + cw_ref[1:2, :] * u1 + cw_ref[2:3, :] * u
    y_ref[:, pl.ds(D_MLSTM, D_CONV)] = (gate_b * conv).astype(y_ref.dtype)
    ucarry_ref[...] = u[ts - 8:ts, :]
    fill()

    mix = _dot(y_ref[...], wo_ref[...])
    x1 = _layer_norm(ALPHA * x + mix, lng_ref[...], lnb_ref[...])
    xo_ref[pl.ds(r0, ts), :] = x1
    xp_ref[pl.ds(r0, ts), :] = _pack_bf16(x1)
    fill(2)

    logits_t = _dot_nt(rw_ref[...], x1.astype(BF16)) + rb_ref[...]
    e0, e1, g0, g1 = _route(logits_t, ts)
    eidx_ref[0:1, pl.ds(r0, ts)] = e0
    eidx_ref[1:2, pl.ds(r0, ts)] = e1
    krow = lax.broadcasted_iota(jnp.int32, (LANES, ts), 0)
    gate_ref[pl.ds(r0, ts), :] = jnp.where(krow == 0, g0, jnp.where(krow == 1, g1, 0.0)).T
    fill(2)

    eid = lax.broadcasted_iota(jnp.int32, (N_EXPERTS, ts), 0)
    hit0 = eid == e0
    hit1 = eid == e1
    onehot = hit0.astype(F32) + hit1.astype(F32)
    t_r = lax.broadcasted_iota(jnp.int32, (ts, ts), 0)
    t_c = lax.broadcasted_iota(jnp.int32, (ts, ts), 1)
    upper = (t_r <= t_c).astype(BF16)
    csum = _dot(onehot.astype(BF16), upper)
    seen = jnp.where(first_tile, 0.0, cnt_scr[...])
    before = csum - onehot + seen[:, 0:1]
    rank_ref[0:1, pl.ds(r0, ts)] = jnp.sum(jnp.where(hit0, before, 0.0), axis=0, keepdims=True).astype(jnp.int32)
    rank_ref[1:2, pl.ds(r0, ts)] = jnp.sum(jnp.where(hit1, before, 0.0), axis=0, keepdims=True).astype(jnp.int32)
    seen = seen + csum[:, ts - 1:ts]
    cnt_scr[...] = seen
    cnt_ref[...] = seen.astype(jnp.int32)


def _mixer_kernel(x_ref, xn_ref, w_ref, b_ref, cw_ref, mhg_ref, wo_ref, lng_ref, lnb_ref,
                  rw_ref, rb_ref,
                  xo_ref, xp_ref, eidx_ref, gate_ref, rank_ref, cnt_ref,
                  pa_ref, pb_ref, y_ref, state_ref, m_ref, ucarry_ref, cnt_scr, *, ts, seq):
    step = pl.program_id(0)
    refs = (cw_ref, mhg_ref, wo_ref, lng_ref, lnb_ref, rw_ref, rb_ref,
            xo_ref, xp_ref, eidx_ref, gate_ref, rank_ref, cnt_ref,
            y_ref, state_ref, m_ref, ucarry_ref, cnt_scr)

    @pl.when(step == 0)
    def _():
        _in_proj(x_ref[0:ts, :], w_ref, b_ref, pa_ref)

    tile_a = 2 * step
    fill = _Filler(_in_proj_pieces(x_ref[ts:2 * ts, :], w_ref, b_ref, pb_ref))
    _mix_tile(pa_ref, x_ref[0:ts, :], 0, (tile_a * ts) % seq == 0, tile_a == 0, refs, ts, fill)
    fill.drain()
    fill = _Filler(_in_proj_pieces(xn_ref[...], w_ref, b_ref, pa_ref))
    _mix_tile(pb_ref, x_ref[ts:2 * ts, :], ts, ((tile_a + 1) * ts) % seq == 0, False, refs, ts, fill)
    fill.drain()


def _const_spec(shape):
    nd = len(shape)
    return pl.BlockSpec(shape, lambda *_: (0,) * nd, pipeline_mode=pl.Buffered(1))


def _mixer_call(x_flat, row0, n, seq, w_all, b_all, conv_w, mh_g, w_out, ln_g, ln_b, rw, rb):
    d = x_flat.shape[1]
    ts = min(MIX_ROWS, seq // 2)
    n_steps = n // (2 * ts)
    step0 = row0 // (2 * ts)
    tok_spec = pl.BlockSpec((TOP_K, 2 * ts), lambda i: (0, i))
    return pl.pallas_call(
        functools.partial(_mixer_kernel, ts=ts, seq=seq),
        grid=(n_steps,),
        in_specs=[
            pl.BlockSpec((2 * ts, d), lambda i: (step0 + i, 0)),
            pl.BlockSpec((ts, d), lambda i: (2 * (step0 + jnp.minimum(i + 1, n_steps - 1)), 0)),
            _const_spec(w_all.shape), _const_spec(b_all.shape), _const_spec(conv_w.shape),
            _const_spec(mh_g.shape), _const_spec(w_out.shape), _const_spec(ln_g.shape),
            _const_spec(ln_b.shape), _const_spec(rw.shape), _const_spec(rb.shape),
        ],
        out_specs=[
            pl.BlockSpec((2 * ts, d), lambda i: (i, 0)),
            pl.BlockSpec((2 * ts, d // 2), lambda i: (i, 0)),
            tok_spec, pl.BlockSpec((2 * ts, LANES), lambda i: (i, 0)), tok_spec,
            pl.BlockSpec((N_EXPERTS, LANES), lambda i: (0, 0)),
        ],
        out_shape=[
            jax.ShapeDtypeStruct((n, d), F32),
            jax.ShapeDtypeStruct((n, d // 2), jnp.uint32),
            jax.ShapeDtypeStruct((TOP_K, n), jnp.int32),
            jax.ShapeDtypeStruct((n, LANES), F32),
            jax.ShapeDtypeStruct((TOP_K, n), jnp.int32),
            jax.ShapeDtypeStruct((N_EXPERTS, LANES), jnp.int32),
        ],
        scratch_shapes=[
            pltpu.VMEM((ts, W_ALL), F32),
            pltpu.VMEM((ts, W_ALL), F32),
            pltpu.VMEM((ts, D_MODEL), BF16),
            pltpu.VMEM((N_HEADS, HEAD_DIM, 2 * HEAD_DIM), F32),
            pltpu.VMEM((8, LANES), F32),
            pltpu.VMEM((8, D_CONV), F32),
            pltpu.VMEM((N_EXPERTS, LANES), F32),
        ],
        compiler_params=pltpu.CompilerParams(
            dimension_semantics=("arbitrary",), vmem_limit_bytes=VMEM_LIMIT),
        name="mixer",
    )(x_flat, x_flat, w_all, b_all, conv_w, mh_g, w_out, ln_g, ln_b, rw, rb)


def _sc_mesh():
    return plsc.VectorSubcoreMesh(core_axis_name="core", subcore_axis_name="subcore")


def _sc_scatter_pair(x_flat, dest, n_slots):
    n, d = x_flat.shape
    windows = [dest[k].reshape(n // SC_WINDOW, 1, SC_WINDOW) for k in range(TOP_K)]

    @functools.partial(
        pl.kernel, out_type=jax.ShapeDtypeStruct((n_slots, d), x_flat.dtype), mesh=_sc_mesh(),
        scratch_types=[])
    def scatter_kernel(x_hbm, *rest):
        idx_hbm, o_hbm = rest[:TOP_K], rest[TOP_K]

        def body(x_vmem, *idx_vmem):
            for i_vmem in idx_vmem:
                pltpu.sync_copy(x_vmem, o_hbm.at[i_vmem.at[0, 0]])

        pltpu.emit_pipeline(
            body,
            grid=(n // SC_WINDOW,),
            in_specs=[pl.BlockSpec((SC_WINDOW, d), lambda i: (i, 0))]
            + [pl.BlockSpec((1, 1, SC_WINDOW), lambda i: (i, 0, 0))] * TOP_K,
            out_specs=[],
            core_axis_name=("core", "subcore"),
            dimension_semantics=(pltpu.PARALLEL,),
        )(x_hbm, *idx_hbm)

    return scatter_kernel(x_flat, *windows)


def _expert_kernel(be_ref, nv_ref, br_ref, buf_ref, wg_ref, wu_ref, wd_ref, y_ref, wgu_scr, wd_scr):
    step = pl.program_id(0)
    used = step < nv_ref[0]

    @pl.when(jnp.logical_and(used, jnp.logical_or(step == 0, be_ref[step] != be_ref[jnp.maximum(step - 1, 0)])))
    def _():
        wgu_scr[:, :D_FF] = wg_ref[0, 0].astype(BF16)
        wgu_scr[:, D_FF:] = wu_ref[0, 0].astype(BF16)
        wd_scr[...] = wd_ref[0, 0].astype(BF16)

    @pl.when(used)
    def _():
        rid = lax.broadcasted_iota(jnp.int32, (MOE_BLOCK, 1), 0)
        words = jnp.where(rid < br_ref[step], buf_ref[...], jnp.uint32(0))
        xb = _unpack_bf16(words).astype(BF16)
        h = _dot(xb, wgu_scr[...])
        hdn = jax.nn.silu(h[:, :D_FF]) * h[:, D_FF:]
        y_ref[...] = _pack_bf16(_dot(hdn.astype(BF16), wd_scr[...]))

    @pl.when(jnp.logical_not(used))
    def _():
        y_ref[...] = jnp.zeros_like(y_ref)


def _expert_call(layer, block_expert, n_valid, block_rows, buf, w_gate, w_up, w_down):
    n_slots, dw = buf.shape
    d = 2 * dw
    n_blocks = n_slots // MOE_BLOCK

    def row_map(i, be, nv, br):
        return (jnp.minimum(i, nv[0] - 1), 0)

    def w_map(i, be, nv, br):
        return (layer, be[jnp.minimum(i, nv[0] - 1)], 0, 0)

    return pl.pallas_call(
        _expert_kernel,
        grid_spec=pltpu.PrefetchScalarGridSpec(
            num_scalar_prefetch=3,
            grid=(n_blocks,),
            in_specs=[
                pl.BlockSpec((MOE_BLOCK, dw), row_map),
                pl.BlockSpec((1, 1, d, D_FF), w_map),
                pl.BlockSpec((1, 1, d, D_FF), w_map),
                pl.BlockSpec((1, 1, D_FF, d), w_map),
            ],
            out_specs=pl.BlockSpec((MOE_BLOCK, dw), lambda i, be, nv, br: (i, 0)),
            scratch_shapes=[pltpu.VMEM((d, 2 * D_FF), BF16), pltpu.VMEM((D_FF, d), BF16)],
        ),
        out_shape=jax.ShapeDtypeStruct((n_slots, dw), jnp.uint32),
        compiler_params=pltpu.CompilerParams(
            dimension_semantics=("arbitrary",), vmem_limit_bytes=VMEM_LIMIT),
        name="experts",
    )(block_expert, n_valid, block_rows, buf, w_gate, w_up, w_down)


def _sc_gather(table, idx):
    m = idx.shape[0]
    d = table.shape[1]

    @functools.partial(
        pl.kernel, out_type=jax.ShapeDtypeStruct((m, d), table.dtype), mesh=_sc_mesh(),
        scratch_types=[])
    def gather_kernel(t_hbm, i_hbm, o_hbm):
        def body(i_vmem, o_vmem):
            pltpu.sync_copy(t_hbm.at[i_vmem.at[0, 0]], o_vmem)

        pltpu.emit_pipeline(
            body,
            grid=(m // SC_WINDOW,),
            in_specs=[pl.BlockSpec((1, 1, SC_WINDOW), lambda i: (i, 0, 0))],
            out_specs=[pl.BlockSpec((SC_WINDOW, d), lambda i: (i, 0))],
            core_axis_name=("core", "subcore"),
            dimension_semantics=(pltpu.PARALLEL,),
        )(i_hbm, o_hbm)

    return gather_kernel(table, idx.reshape(m // SC_WINDOW, 1, SC_WINDOW))


def _combine_kernel(x_ref, y0_ref, y1_ref, gate_ref, lng_ref, lnb_ref, *rest):
    o_ref = rest[-1]
    ffn = _unpack_bf16(y0_ref[...]) * gate_ref[:, 0:1] + _unpack_bf16(y1_ref[...]) * gate_ref[:, 1:2]
    o_ref[...] = _layer_norm(ALPHA * x_ref[...] + ffn, lng_ref[...], lnb_ref[...])


def _combine_call(x_flat, dest, gates_t, y_buf, ln_g, ln_b, out_rows=None, out_row0=0, out_prev=None):
    n, d = x_flat.shape
    rows = min(2 * ROW_TILE, n)
    nb = n // rows
    b0 = out_row0 // rows
    yg = _sc_gather(y_buf, dest.reshape(TOP_K * n))
    operands = [x_flat, yg, yg, gates_t, ln_g, ln_b]
    in_specs = [
        pl.BlockSpec((rows, d), lambda i: (i, 0)),
        pl.BlockSpec((rows, d // 2), lambda i: (i, 0)),
        pl.BlockSpec((rows, d // 2), lambda i: (i + nb, 0)),
        pl.BlockSpec((rows, LANES), lambda i: (i, 0)),
        _const_spec(ln_g.shape), _const_spec(ln_b.shape),
    ]
    aliases = {}
    if out_prev is not None:
        operands.append(out_prev)
        in_specs.append(pl.BlockSpec(memory_space=pl.ANY))
        aliases = {6: 0}
    return pl.pallas_call(
        _combine_kernel,
        grid=(nb,),
        in_specs=in_specs,
        out_specs=pl.BlockSpec((rows, d), lambda i: (b0 + i, 0)),
        out_shape=jax.ShapeDtypeStruct((out_rows or n, d), F32),
        input_output_aliases=aliases,
        compiler_params=pltpu.CompilerParams(
            dimension_semantics=("arbitrary",), vmem_limit_bytes=VMEM_LIMIT),
        name="combine",
    )(*operands)


def _slot_plan(counts, eidx, rank, n_blocks):
    padded = (counts + MOE_BLOCK - 1) // MOE_BLOCK * MOE_BLOCK
    pad_end = jnp.cumsum(padded)
    pad_start = pad_end - padded
    dest = rank
    for e in range(N_EXPERTS):
        dest = dest + jnp.where(eidx == e, pad_start[e], 0)
    block_row0 = jnp.arange(n_blocks, dtype=jnp.int32) * MOE_BLOCK
    block_expert = jnp.minimum(
        jnp.sum(pad_end[None, :] <= block_row0[:, None], axis=1), N_EXPERTS - 1).astype(jnp.int32)
    n_valid = (pad_end[-1:] // MOE_BLOCK).astype(jnp.int32)
    block_rows = jnp.clip(
        (pad_start + counts)[block_expert] - block_row0, 0, MOE_BLOCK).astype(jnp.int32)
    return dest.astype(jnp.int32), block_expert, n_valid, block_rows


def kernel(x, w_in, b_in, conv_w, mh_norm_g, w_out, ln_mix_g, ln_mix_b, router_w, router_b,
           w_gate, w_up, w_down, ln_moe_g, ln_moe_b):
    bsz, seq, d = x.shape
    n = bsz * seq

    gate_pad = jnp.zeros(w_in.shape[:2] + (LANES - N_HEADS,), w_in.dtype)
    w_all = jnp.concatenate(
        [w_in[..., OFF_Q:OFF_I], w_in[..., OFF_B:P_IN], w_in[..., OFF_I:OFF_F], gate_pad,
         w_in[..., OFF_F:OFF_B], gate_pad], axis=-1).astype(BF16)
    b_all = jnp.concatenate(
        [b_in[..., OFF_Q:OFF_I], b_in[..., OFF_B:P_IN], b_in[..., OFF_I:OFF_F], gate_pad[:, 0, :],
         b_in[..., OFF_F:OFF_B], gate_pad[:, 0, :]], axis=-1).astype(F32)[:, None, :]
    w_out_b = w_out.astype(BF16)
    rw = router_w.T.astype(BF16)
    rb = router_b.astype(F32)[:, None]

    n_chain = N_CHAINS if bsz % N_CHAINS == 0 else 1
    nc = n // n_chain
    n_slots = nc * TOP_K + N_EXPERTS * MOE_BLOCK
    n_blocks = n_slots // MOE_BLOCK
    x_full = x.reshape(n, d)
    acts = [(x_full, c * nc) for c in range(n_chain)]
    out = None
    for l in range(DEPTH):
        last = l == DEPTH - 1
        for c in range(n_chain):
            xin, row0 = acts[c]
            x1, x1p, eidx, gates, rank, cnt = _mixer_call(
                xin, row0, nc, seq, w_all[l], b_all[l], conv_w[l], mh_norm_g[l][None, :], w_out_b[l],
                ln_mix_g[l][None, :], ln_mix_b[l][None, :], rw, rb)
            dest, block_expert, n_valid, block_rows = _slot_plan(cnt[:, 0], eidx, rank, n_blocks)
            buf = _sc_scatter_pair(x1p, dest, n_slots)
            y_buf = _expert_call(l, block_expert, n_valid, block_rows, buf, w_gate, w_up, w_down)
            if last:
                out = _combine_call(x1, dest, gates, y_buf, ln_moe_g[l][None, :], ln_moe_b[l][None, :],
                                    out_rows=n, out_row0=c * nc, out_prev=out)
            else:
                acts[c] = (_combine_call(x1, dest, gates, y_buf, ln_moe_g[l][None, :],
                                         ln_moe_b[l][None, :]), 0)
    return out.reshape(bsz, seq, d)
```

```python
import functools

import jax
import jax.numpy as jnp
from jax import lax
from jax.experimental import pallas as pl
from jax.experimental.pallas import tpu as pltpu
from jax.experimental.pallas import tpu_sc as plsc

D_MODEL = 1024
DEPTH = 4
D_MLSTM = D_MODEL // 2
D_CONV = D_MODEL - D_MLSTM
N_HEADS = 4
HEAD_DIM = D_MLSTM // N_HEADS
CONV_WIDTH = 3
CHUNK = 128
N_EXPERTS = 16
N_GROUPS = 4
EXPERTS_PER_GROUP = N_EXPERTS // N_GROUPS
TOP_K = 2
D_FF = D_MODEL // 2
LN_EPS = 1e-5
ALPHA = (2 * DEPTH) ** 0.25

OFF_Q = 0
OFF_K = OFF_Q + D_MLSTM
OFF_V = OFF_K + D_MLSTM
OFF_O = OFF_V + D_MLSTM
OFF_I = OFF_O + D_MLSTM
OFF_F = OFF_I + N_HEADS
OFF_B = OFF_F + N_HEADS
OFF_C = OFF_B + D_CONV
OFF_X = OFF_C + D_CONV
P_IN = OFF_X + D_CONV

LANES = 128
W_M = 4 * D_MLSTM
W_C = 3 * D_CONV
GATE_I = W_M + W_C
GATE_F = GATE_I + LANES
W_ALL = GATE_F + LANES

MIX_ROWS = 512
PIECE_COLS = 256
MOE_BLOCK = 1024
ROW_TILE = 256
N_CHAINS = 1
SC_WINDOW = 64
VMEM_LIMIT = 56 * 1024 * 1024

F32 = jnp.float32
BF16 = jnp.bfloat16


def _dot(a, b):
    return jnp.dot(a, b, preferred_element_type=F32)


def _dot_nt(a, b):
    return lax.dot_general(a, b, (((1,), (1,)), ((), ())), preferred_element_type=F32)


def _split_bf16(x):
    hi = x.astype(BF16)
    lo = (x - hi.astype(F32)).astype(BF16)
    return hi, lo


def _pack_bf16(x):
    c = x.shape[1] // 2
    bits = lax.bitcast_convert_type(x.astype(BF16).astype(F32), jnp.uint32)
    return (bits[:, c:] & jnp.uint32(0xFFFF0000)) | (bits[:, :c] >> 16)


def _unpack_bf16(w):
    lo = lax.bitcast_convert_type(w << 16, F32)
    hi = lax.bitcast_convert_type(w & jnp.uint32(0xFFFF0000), F32)
    return jnp.concatenate([lo, hi], axis=1)


def _layer_norm(z, g, b):
    mu = jnp.mean(z, axis=-1, keepdims=True)
    d = z - mu
    var = jnp.mean(d * d, axis=-1, keepdims=True)
    return d * lax.rsqrt(var + LN_EPS) * g + b


def _rows(x, c):
    return x[c * CHUNK:(c + 1) * CHUNK]


def _stack(parts):
    return parts[0] if len(parts) == 1 else jnp.concatenate(parts, axis=0)


def _mlstm_tile(p_ref, y_ref, state_ref, m_ref, mhg_ref, fresh, ts, fill):
    L = CHUNK
    nch = ts // L
    row_i = lax.broadcasted_iota(jnp.int32, (L, L), 0)
    col_i = lax.broadcasted_iota(jnp.int32, (L, L), 1)
    tri = (col_i <= row_i).astype(BF16)
    causal = _stack([col_i <= row_i] * nch)
    ones = jnp.ones((ts, LANES), BF16)

    def per_chunk(vals):
        return _stack([jnp.broadcast_to(vals[c], (L, LANES)) for c in range(nch)])

    ig = p_ref[:, pl.ds(GATE_I, LANES)]
    fg = p_ref[:, pl.ds(GATE_F, LANES)]
    logf = jnp.minimum(fg, 0.0) - jnp.log1p(jnp.exp(-jnp.abs(fg)))
    lf_hi, lf_lo = _split_bf16(logf)
    a = _stack([_dot(tri, _rows(lf_hi, c)) + _dot(tri, _rows(lf_lo, c)) for c in range(nch)])
    b_t = [(_rows(ig, c) - _rows(a, c)).T for c in range(nch)]
    fill()

    for h in range(N_HEADS):
        c0 = h * HEAD_DIM
        q = p_ref[:, pl.ds(OFF_Q + c0, HEAD_DIM)]
        k = p_ref[:, pl.ds(OFF_K + c0, HEAD_DIM)] * (HEAD_DIM ** -0.5)
        v = p_ref[:, pl.ds(OFF_V + c0, HEAD_DIM)]
        o = p_ref[:, pl.ds(OFF_O + c0, HEAD_DIM)]

        a_b = jnp.broadcast_to(a[:, h:h + 1], (ts, LANES))
        ig_b = jnp.broadcast_to(ig[:, h:h + 1], (ts, LANES))
        g = [a_b[(c + 1) * L - 1:(c + 1) * L, :] for c in range(nch)]
        w_end = per_chunk(g) - a_b + ig_b
        m_loc = [jnp.max(_rows(w_end, c), axis=0, keepdims=True) for c in range(nch)]
        m = [jnp.where(fresh, 0.0, m_ref[h:h + 1, :])]
        for c in range(nch):
            m.append(jnp.maximum(g[c] + m[c], m_loc[c]))
        m_ref[h:h + 1, :] = m[nch]
        log_inter = a_b + per_chunk(m)
        e_end = jnp.exp(w_end - per_chunk(m_loc))

        b_rows = _stack([jnp.broadcast_to(b_t[c][h:h + 1, :], (L, L)) for c in range(nch)])
        log_d = jnp.where(causal, a_b + b_rows, -jnp.inf)
        m_out = jnp.maximum(log_inter, jnp.max(log_d, axis=-1, keepdims=True))
        dmat = jnp.exp(log_d - m_out)
        if h < 2:
            fill()
        qb = q.astype(BF16)
        kb = k.astype(BF16)
        v_aug = jnp.concatenate([v.astype(BF16), ones], axis=1)
        s_qk = (_stack([_dot_nt(_rows(qb, c), _rows(kb, c)) for c in range(nch)]) * dmat).astype(BF16)
        intra = _stack([_dot(_rows(s_qk, c), _rows(v_aug, c)) for c in range(nch)])
        k_e = k * e_end
        s_loc = [_dot(_rows(k_e, c).T.astype(BF16), _rows(v_aug, c)) for c in range(nch)]
        state = [jnp.where(fresh, 0.0, state_ref[h])]
        for c in range(nch):
            keep = jnp.exp(g[c] + m[c] - m[c + 1])
            take = jnp.exp(m_loc[c] - m[c + 1])
            state.append(jnp.concatenate([keep, keep], axis=1) * state[c]
                         + jnp.concatenate([take, take], axis=1) * s_loc[c])
        state_ref[h] = state[nch]
        inter = _stack([_dot(_rows(qb, c), state[c].astype(BF16)) for c in range(nch)])
        e_inter = jnp.exp(log_inter - m_out)
        num = intra[:, :HEAD_DIM] + e_inter * inter[:, :HEAD_DIM]
        den = intra[:, HEAD_DIM:] + e_inter * inter[:, HEAD_DIM:]
        hh = num / jnp.maximum(jnp.abs(den), jnp.exp(-m_out))

        mu = jnp.mean(hh, axis=-1, keepdims=True)
        dh = hh - mu
        var = jnp.mean(dh * dh, axis=-1, keepdims=True)
        hn = dh * lax.rsqrt(var + LN_EPS) * mhg_ref[:, pl.ds(c0, HEAD_DIM)]
        y_ref[:, pl.ds(c0, HEAD_DIM)] = (jax.nn.sigmoid(o) * hn).astype(y_ref.dtype)
        fill()


def _route(logits_t, ts, fill):
    lg = [logits_t[e:e + 1, :] for e in range(N_EXPERTS)]
    mx = functools.reduce(jnp.maximum, lg)
    ex = [jnp.exp(v - mx) for v in lg]
    tot = functools.reduce(lambda a, b: a + b, ex)
    p = [v / tot for v in ex]
    fill()
    scores = []
    for g in range(N_GROUPS):
        p0, p1, p2, p3 = p[g * EXPERTS_PER_GROUP:(g + 1) * EXPERTS_PER_GROUP]
        hi01, lo01 = jnp.maximum(p0, p1), jnp.minimum(p0, p1)
        hi23, lo23 = jnp.maximum(p2, p3), jnp.minimum(p2, p3)
        top1 = jnp.maximum(hi01, hi23)
        top2 = jnp.maximum(jnp.minimum(hi01, hi23), jnp.maximum(lo01, lo23))
        scores.append(top1 + top2)
    grp = jnp.zeros((1, ts), jnp.int32)
    best = scores[0]
    for g in range(1, N_GROUPS):
        upd = scores[g] > best
        best = jnp.where(upd, scores[g], best)
        grp = jnp.where(upd, g, grp)
    fill()
    sel = []
    for j in range(EXPERTS_PER_GROUP):
        v = p[j]
        for g in range(1, N_GROUPS):
            v = jnp.where(grp == g, p[g * EXPERTS_PER_GROUP + j], v)
        sel.append(v)
    i1 = jnp.zeros((1, ts), jnp.int32)
    v1 = sel[0]
    for j in range(1, EXPERTS_PER_GROUP):
        upd = sel[j] > v1
        v1 = jnp.where(upd, sel[j], v1)
        i1 = jnp.where(upd, j, i1)
    fill()
    i2 = jnp.zeros((1, ts), jnp.int32)
    v2 = jnp.full((1, ts), -jnp.inf, F32)
    for j in range(EXPERTS_PER_GROUP):
        cand = jnp.where(i1 == j, -jnp.inf, sel[j])
        upd = cand > v2
        v2 = jnp.where(upd, cand, v2)
        i2 = jnp.where(upd, j, i2)
    e0 = grp * EXPERTS_PER_GROUP + i1
    e1 = grp * EXPERTS_PER_GROUP + i2
    s12 = v1 + v2
    return e0, e1, v1 / s12, v2 / s12


def _in_proj_pieces(x, w_ref, b_ref, p_ref):
    xb = x.astype(BF16)

    def piece(j):
        cols = pl.ds(j * PIECE_COLS, PIECE_COLS)
        p_ref[:, cols] = _dot(xb, w_ref[:, cols]) + b_ref[:, cols]

    return [functools.partial(piece, j) for j in range(W_ALL // PIECE_COLS)]


def _in_proj(x, w_ref, b_ref, p_ref):
    for piece in _in_proj_pieces(x, w_ref, b_ref, p_ref):
        piece()


class _Filler:
    def __init__(self, thunks):
        self._thunks = list(thunks)

    def __call__(self, count=1):
        for _ in range(count):
            if self._thunks:
                self._thunks.pop(0)()

    def drain(self):
        self(len(self._thunks))


def _mix_tile(p_ref, x, r0, fresh, first_tile, refs, ts, fill):
    (cw_ref, mhg_ref, wo_ref, lng_ref, lnb_ref, rw_ref, rb_ref,
     xo_ref, xp_ref, eidx_ref, gate_ref, rank_ref, cnt_ref,
     y_ref, state_ref, m_ref, ucarry_ref, cnt_scr) = refs
    _mlstm_tile(p_ref, y_ref, state_ref, m_ref, mhg_ref, fresh, ts, fill)

    gate_b = p_ref[:, pl.ds(W_M, D_CONV)]
    u = p_ref[:, pl.ds(W_M + D_CONV, D_CONV)] * p_ref[:, pl.ds(W_M + 2 * D_CONV, D_CONV)]
    rid = lax.broadcasted_iota(jnp.int32, (ts, D_CONV), 0)
    prev1 = jnp.where(fresh, 0.0, ucarry_ref[7:8, :])
    prev2 = jnp.where(fresh, 0.0, ucarry_ref[6:7, :])
    u1 = jnp.where(rid == 0, prev1, pltpu.roll(u, 1, 0))
    u2 = jnp.where(rid == 0, prev2, jnp.where(rid == 1, prev1, pltpu.roll(u, 2, 0)))
    conv = cw_ref[0:1, :] * u2 + cw_ref[1:2, :] * u1 + cw_ref[2:3, :] * u
    y_ref[:, pl.ds(D_MLSTM, D_CONV)] = (gate_b * conv).astype(y_ref.dtype)
    ucarry_ref[...] = u[ts - 8:ts, :]
    fill()

    mix = _dot(y_ref[...], wo_ref[...])
    x1 = _layer_norm(ALPHA * x + mix, lng_ref[...], lnb_ref[...])
    xo_ref[pl.ds(r0, ts), :] = x1
    xp_ref[pl.ds(r0, ts), :] = _pack_bf16(x1)
    fill()

    logits_t = _dot_nt(rw_ref[...], x1.astype(BF16)) + rb_ref[...]
    e0, e1, g0, g1 = _route(logits_t, ts, fill)
    eidx_ref[0:1, pl.ds(r0, ts)] = e0
    eidx_ref[1:2, pl.ds(r0, ts)] = e1
    krow = lax.broadcasted_iota(jnp.int32, (LANES, ts), 0)
    gate_ref[pl.ds(r0, ts), :] = jnp.where(krow == 0, g0, jnp.where(krow == 1, g1, 0.0)).T
    fill()

    eid = lax.broadcasted_iota(jnp.int32, (N_EXPERTS, ts), 0)
    hit0 = eid == e0
    hit1 = eid == e1
    onehot = hit0.astype(F32) + hit1.astype(F32)
    t_r = lax.broadcasted_iota(jnp.int32, (ts, ts), 0)
    t_c = lax.broadcasted_iota(jnp.int32, (ts, ts), 1)
    upper = (t_r <= t_c).astype(BF16)
    csum = _dot(onehot.astype(BF16), upper)
    seen = jnp.where(first_tile, 0.0, cnt_scr[...])
    before = csum - onehot + seen[:, 0:1]
    rank_ref[0:1, pl.ds(r0, ts)] = jnp.sum(jnp.where(hit0, before, 0.0), axis=0, keepdims=True).astype(jnp.int32)
    rank_ref[1:2, pl.ds(r0, ts)] = jnp.sum(jnp.where(hit1, before, 0.0), axis=0, keepdims=True).astype(jnp.int32)
    seen = seen + csum[:, ts - 1:ts]
    cnt_scr[...] = seen
    cnt_ref[...] = seen.astype(jnp.int32)


def _mixer_kernel(x_ref, xn_ref, w_ref, b_ref, cw_ref, mhg_ref, wo_ref, lng_ref, lnb_ref,
                  rw_ref, rb_ref,
                  xo_ref, xp_ref, eidx_ref, gate_ref, rank_ref, cnt_ref,
                  pa_ref, pb_ref, y_ref, state_ref, m_ref, ucarry_ref, cnt_scr, *, ts, seq):
    step = pl.program_id(0)
    refs = (cw_ref, mhg_ref, wo_ref, lng_ref, lnb_ref, rw_ref, rb_ref,
            xo_ref, xp_ref, eidx_ref, gate_ref, rank_ref, cnt_ref,
            y_ref, state_ref, m_ref, ucarry_ref, cnt_scr)

    @pl.when(step == 0)
    def _():
        _in_proj(x_ref[0:ts, :], w_ref, b_ref, pa_ref)

    tile_a = 2 * step
    fill = _Filler(_in_proj_pieces(x_ref[ts:2 * ts, :], w_ref, b_ref, pb_ref))
    _mix_tile(pa_ref, x_ref[0:ts, :], 0, (tile_a * ts) % seq == 0, tile_a == 0, refs, ts, fill)
    fill.drain()
    fill = _Filler(_in_proj_pieces(xn_ref[...], w_ref, b_ref, pa_ref))
    _mix_tile(pb_ref, x_ref[ts:2 * ts, :], ts, ((tile_a + 1) * ts) % seq == 0, False, refs, ts, fill)
    fill.drain()


def _const_spec(shape):
    nd = len(shape)
    return pl.BlockSpec(shape, lambda *_: (0,) * nd, pipeline_mode=pl.Buffered(1))


def _mixer_call(x_flat, row0, n, seq, w_all, b_all, conv_w, mh_g, w_out, ln_g, ln_b, rw, rb):
    d = x_flat.shape[1]
    ts = min(MIX_ROWS, seq // 2)
    n_steps = n // (2 * ts)
    step0 = row0 // (2 * ts)
    tok_spec = pl.BlockSpec((TOP_K, 2 * ts), lambda i: (0, i))
    return pl.pallas_call(
        functools.partial(_mixer_kernel, ts=ts, seq=seq),
        grid=(n_steps,),
        in_specs=[
            pl.BlockSpec((2 * ts, d), lambda i: (step0 + i, 0)),
            pl.BlockSpec((ts, d), lambda i: (2 * (step0 + jnp.minimum(i + 1, n_steps - 1)), 0)),
            _const_spec(w_all.shape), _const_spec(b_all.shape), _const_spec(conv_w.shape),
            _const_spec(mh_g.shape), _const_spec(w_out.shape), _const_spec(ln_g.shape),
            _const_spec(ln_b.shape), _const_spec(rw.shape), _const_spec(rb.shape),
        ],
        out_specs=[
            pl.BlockSpec((2 * ts, d), lambda i: (i, 0)),
            pl.BlockSpec((2 * ts, d // 2), lambda i: (i, 0)),
            tok_spec, pl.BlockSpec((2 * ts, LANES), lambda i: (i, 0)), tok_spec,
            pl.BlockSpec((N_EXPERTS, LANES), lambda i: (0, 0)),
        ],
        out_shape=[
            jax.ShapeDtypeStruct((n, d), F32),
            jax.ShapeDtypeStruct((n, d // 2), jnp.uint32),
            jax.ShapeDtypeStruct((TOP_K, n), jnp.int32),
            jax.ShapeDtypeStruct((n, LANES), F32),
            jax.ShapeDtypeStruct((TOP_K, n), jnp.int32),
            jax.ShapeDtypeStruct((N_EXPERTS, LANES), jnp.int32),
        ],
        scratch_shapes=[
            pltpu.VMEM((ts, W_ALL), F32),
            pltpu.VMEM((ts, W_ALL), F32),
            pltpu.VMEM((ts, D_MODEL), BF16),
            pltpu.VMEM((N_HEADS, HEAD_DIM, 2 * HEAD_DIM), F32),
            pltpu.VMEM((8, LANES), F32),
            pltpu.VMEM((8, D_CONV), F32),
            pltpu.VMEM((N_EXPERTS, LANES), F32),
        ],
        compiler_params=pltpu.CompilerParams(
            dimension_semantics=("arbitrary",), vmem_limit_bytes=VMEM_LIMIT),
        name="mixer",
    )(x_flat, x_flat, w_all, b_all, conv_w, mh_g, w_out, ln_g, ln_b, rw, rb)


def _sc_mesh():
    return plsc.VectorSubcoreMesh(core_axis_name="core", subcore_axis_name="subcore")


def _sc_scatter_pair(x_flat, dest, n_slots):
    n, d = x_flat.shape
    windows = [dest[k].reshape(n // SC_WINDOW, 1, SC_WINDOW) for k in range(TOP_K)]

    @functools.partial(
        pl.kernel, out_type=jax.ShapeDtypeStruct((n_slots, d), x_flat.dtype), mesh=_sc_mesh(),
        scratch_types=[])
    def scatter_kernel(x_hbm, *rest):
        idx_hbm, o_hbm = rest[:TOP_K], rest[TOP_K]

        def body(x_vmem, *idx_vmem):
            for i_vmem in idx_vmem:
                pltpu.sync_copy(x_vmem, o_hbm.at[i_vmem.at[0, 0]])

        pltpu.emit_pipeline(
            body,
            grid=(n // SC_WINDOW,),
            in_specs=[pl.BlockSpec((SC_WINDOW, d), lambda i: (i, 0))]
            + [pl.BlockSpec((1, 1, SC_WINDOW), lambda i: (i, 0, 0))] * TOP_K,
            out_specs=[],
            core_axis_name=("core", "subcore"),
            dimension_semantics=(pltpu.PARALLEL,),
        )(x_hbm, *idx_hbm)

    return scatter_kernel(x_flat, *windows)


def _expert_kernel(be_ref, nv_ref, br_ref, buf_ref, wg_ref, wu_ref, wd_ref, y_ref, wgu_scr, wd_scr):
    step = pl.program_id(0)
    used = step < nv_ref[0]

    @pl.when(jnp.logical_and(used, jnp.logical_or(step == 0, be_ref[step] != be_ref[jnp.maximum(step - 1, 0)])))
    def _():
        wgu_scr[:, :D_FF] = wg_ref[0, 0].astype(BF16)
        wgu_scr[:, D_FF:] = wu_ref[0, 0].astype(BF16)
        wd_scr[...] = wd_ref[0, 0].astype(BF16)

    @pl.when(used)
    def _():
        rid = lax.broadcasted_iota(jnp.int32, (MOE_BLOCK, 1), 0)
        words = jnp.where(rid < br_ref[step], buf_ref[...], jnp.uint32(0))
        xb = _unpack_bf16(words).astype(BF16)
        h = _dot(xb, wgu_scr[...])
        hdn = jax.nn.silu(h[:, :D_FF]) * h[:, D_FF:]
        y_ref[...] = _pack_bf16(_dot(hdn.astype(BF16), wd_scr[...]))

    @pl.when(jnp.logical_not(used))
    def _():
        y_ref[...] = jnp.zeros_like(y_ref)


def _expert_call(layer, block_expert, n_valid, block_rows, buf, w_gate, w_up, w_down):
    n_slots, dw = buf.shape
    d = 2 * dw
    n_blocks = n_slots // MOE_BLOCK

    def row_map(i, be, nv, br):
        return (jnp.minimum(i, nv[0] - 1), 0)

    def w_map(i, be, nv, br):
        return (layer, be[jnp.minimum(i, nv[0] - 1)], 0, 0)

    return pl.pallas_call(
        _expert_kernel,
        grid_spec=pltpu.PrefetchScalarGridSpec(
            num_scalar_prefetch=3,
            grid=(n_blocks,),
            in_specs=[
                pl.BlockSpec((MOE_BLOCK, dw), row_map),
                pl.BlockSpec((1, 1, d, D_FF), w_map),
                pl.BlockSpec((1, 1, d, D_FF), w_map),
                pl.BlockSpec((1, 1, D_FF, d), w_map),
            ],
            out_specs=pl.BlockSpec((MOE_BLOCK, dw), lambda i, be, nv, br: (i, 0)),
            scratch_shapes=[pltpu.VMEM((d, 2 * D_FF), BF16), pltpu.VMEM((D_FF, d), BF16)],
        ),
        out_shape=jax.ShapeDtypeStruct((n_slots, dw), jnp.uint32),
        compiler_params=pltpu.CompilerParams(
            dimension_semantics=("arbitrary",), vmem_limit_bytes=VMEM_LIMIT),
        name="experts",
    )(block_expert, n_valid, block_rows, buf, w_gate, w_up, w_down)


def _sc_gather(table, idx):
    m = idx.shape[0]
    d = table.shape[1]

    @functools.partial(
        pl.kernel, out_type=jax.ShapeDtypeStruct((m, d), table.dtype), mesh=_sc_mesh(),
        scratch_types=[])
    def gather_kernel(t_hbm, i_hbm, o_hbm):
        def body(i_vmem, o_vmem):
            pltpu.sync_copy(t_hbm.at[i_vmem.at[0, 0]], o_vmem)

        pltpu.emit_pipeline(
            body,
            grid=(m // SC_WINDOW,),
            in_specs=[pl.BlockSpec((1, 1, SC_WINDOW), lambda i: (i, 0, 0))],
            out_specs=[pl.BlockSpec((SC_WINDOW, d), lambda i: (i, 0))],
            core_axis_name=("core", "subcore"),
            dimension_semantics=(pltpu.PARALLEL,),
        )(i_hbm, o_hbm)

    return gather_kernel(table, idx.reshape(m // SC_WINDOW, 1, SC_WINDOW))


def _combine_kernel(x_ref, y0_ref, y1_ref, gate_ref, lng_ref, lnb_ref, *rest):
    o_ref = rest[-1]
    ffn = _unpack_bf16(y0_ref[...]) * gate_ref[:, 0:1] + _unpack_bf16(y1_ref[...]) * gate_ref[:, 1:2]
    o_ref[...] = _layer_norm(ALPHA * x_ref[...] + ffn, lng_ref[...], lnb_ref[...])


def _combine_call(x_flat, dest, gates_t, y_buf, ln_g, ln_b, out_rows=None, out_row0=0, out_prev=None):
    n, d = x_flat.shape
    rows = min(2 * ROW_TILE, n)
    nb = n // rows
    b0 = out_row0 // rows
    yg = _sc_gather(y_buf, dest.reshape(TOP_K * n))
    operands = [x_flat, yg, yg, gates_t, ln_g, ln_b]
    in_specs = [
        pl.BlockSpec((rows, d), lambda i: (i, 0)),
        pl.BlockSpec((rows, d // 2), lambda i: (i, 0)),
        pl.BlockSpec((rows, d // 2), lambda i: (i + nb, 0)),
        pl.BlockSpec((rows, LANES), lambda i: (i, 0)),
        _const_spec(ln_g.shape), _const_spec(ln_b.shape),
    ]
    aliases = {}
    if out_prev is not None:
        operands.append(out_prev)
        in_specs.append(pl.BlockSpec(memory_space=pl.ANY))
        aliases = {6: 0}
    return pl.pallas_call(
        _combine_kernel,
        grid=(nb,),
        in_specs=in_specs,
        out_specs=pl.BlockSpec((rows, d), lambda i: (b0 + i, 0)),
        out_shape=jax.ShapeDtypeStruct((out_rows or n, d), F32),
        input_output_aliases=aliases,
        compiler_params=pltpu.CompilerParams(
            dimension_semantics=("arbitrary",), vmem_limit_bytes=VMEM_LIMIT),
        name="combine",
    )(*operands)


def _slot_plan(counts, eidx, rank, n_blocks):
    padded = (counts + MOE_BLOCK - 1) // MOE_BLOCK * MOE_BLOCK
    pad_end = jnp.cumsum(padded)
    pad_start = pad_end - padded
    dest = rank
    for e in range(N_EXPERTS):
        dest = dest + jnp.where(eidx == e, pad_start[e], 0)
    block_row0 = jnp.arange(n_blocks, dtype=jnp.int32) * MOE_BLOCK
    block_expert = jnp.minimum(
        jnp.sum(pad_end[None, :] <= block_row0[:, None], axis=1), N_EXPERTS - 1).astype(jnp.int32)
    n_valid = (pad_end[-1:] // MOE_BLOCK).astype(jnp.int32)
    block_rows = jnp.clip(
        (pad_start + counts)[block_expert] - block_row0, 0, MOE_BLOCK).astype(jnp.int32)
    return dest.astype(jnp.int32), block_expert, n_valid, block_rows


def kernel(x, w_in, b_in, conv_w, mh_norm_g, w_out, ln_mix_g, ln_mix_b, router_w, router_b,
           w_gate, w_up, w_down, ln_moe_g, ln_moe_b):
    bsz, seq, d = x.shape
    n = bsz * seq

    gate_pad = jnp.zeros(w_in.shape[:2] + (LANES - N_HEADS,), w_in.dtype)
    w_all = jnp.concatenate(
        [w_in[..., OFF_Q:OFF_I], w_in[..., OFF_B:P_IN], w_in[..., OFF_I:OFF_F], gate_pad,
         w_in[..., OFF_F:OFF_B], gate_pad], axis=-1).astype(BF16)
    b_all = jnp.concatenate(
        [b_in[..., OFF_Q:OFF_I], b_in[..., OFF_B:P_IN], b_in[..., OFF_I:OFF_F], gate_pad[:, 0, :],
         b_in[..., OFF_F:OFF_B], gate_pad[:, 0, :]], axis=-1).astype(F32)[:, None, :]
    w_out_b = w_out.astype(BF16)
    rw = router_w.T.astype(BF16)
    rb = router_b.astype(F32)[:, None]

    n_chain = N_CHAINS if bsz % N_CHAINS == 0 else 1
    nc = n // n_chain
    n_slots = nc * TOP_K + N_EXPERTS * MOE_BLOCK
    n_blocks = n_slots // MOE_BLOCK
    x_full = x.reshape(n, d)
    acts = [(x_full, c * nc) for c in range(n_chain)]
    out = None
    for l in range(DEPTH):
        last = l == DEPTH - 1
        for c in range(n_chain):
            xin, row0 = acts[c]
            x1, x1p, eidx, gates, rank, cnt = _mixer_call(
                xin, row0, nc, seq, w_all[l], b_all[l], conv_w[l], mh_norm_g[l][None, :], w_out_b[l],
                ln_mix_g[l][None, :], ln_mix_b[l][None, :], rw, rb)
            dest, block_expert, n_valid, block_rows = _slot_plan(cnt[:, 0], eidx, rank, n_blocks)
            buf = _sc_scatter_pair(x1p, dest, n_slots)
            y_buf = _expert_call(l, block_expert, n_valid, block_rows, buf, w_gate, w_up, w_down)
            if last:
                out = _combine_call(x1, dest, gates, y_buf, ln_moe_g[l][None, :], ln_moe_b[l][None, :],
                                    out_rows=n, out_row0=c * nc, out_prev=out)
            else:
                acts[c] = (_combine_call(x1, dest, gates, y_buf, ln_moe_g[l][None, :],
                                         ln_moe_b[l][None, :]), 0)
    return out.reshape(bsz, seq, d)
```

```python
import functools

import jax
import jax.numpy as jnp
from jax import lax
from jax.experimental import pallas as pl
from jax.experimental.pallas import tpu as pltpu
from jax.experimental.pallas import tpu_sc as plsc

D_MODEL = 1024
DEPTH = 4
D_MLSTM = D_MODEL // 2
D_CONV = D_MODEL - D_MLSTM
N_HEADS = 4
HEAD_DIM = D_MLSTM // N_HEADS
CONV_WIDTH = 3
CHUNK = 128
N_EXPERTS = 16
N_GROUPS = 4
EXPERTS_PER_GROUP = N_EXPERTS // N_GROUPS
TOP_K = 2
D_FF = D_MODEL // 2
LN_EPS = 1e-5
ALPHA = (2 * DEPTH) ** 0.25

OFF_Q = 0
OFF_K = OFF_Q + D_MLSTM
OFF_V = OFF_K + D_MLSTM
OFF_O = OFF_V + D_MLSTM
OFF_I = OFF_O + D_MLSTM
OFF_F = OFF_I + N_HEADS
OFF_B = OFF_F + N_HEADS
OFF_C = OFF_B + D_CONV
OFF_X = OFF_C + D_CONV
P_IN = OFF_X + D_CONV

LANES = 128
W_M = 4 * D_MLSTM
W_C = 3 * D_CONV
GATE_I = W_M + W_C
GATE_F = GATE_I + LANES
W_ALL = GATE_F + LANES

MIX_ROWS = 512
PIECE_COLS = 256
MOE_BLOCK = 1024
COMBINE_ROWS = 1024
SC_WINDOW = 64
VMEM_LIMIT = 56 * 1024 * 1024

F32 = jnp.float32
BF16 = jnp.bfloat16


def _dot(a, b):
    return jnp.dot(a, b, preferred_element_type=F32)


def _dot_nt(a, b):
    return lax.dot_general(a, b, (((1,), (1,)), ((), ())), preferred_element_type=F32)


def _split_bf16(x):
    hi = x.astype(BF16)
    lo = (x - hi.astype(F32)).astype(BF16)
    return hi, lo


def _pack_bf16(x):
    c = x.shape[1] // 2
    bits = lax.bitcast_convert_type(x.astype(BF16).astype(F32), jnp.uint32)
    return (bits[:, c:] & jnp.uint32(0xFFFF0000)) | (bits[:, :c] >> 16)


def _unpack_bf16(w):
    lo = lax.bitcast_convert_type(w << 16, F32)
    hi = lax.bitcast_convert_type(w & jnp.uint32(0xFFFF0000), F32)
    return jnp.concatenate([lo, hi], axis=1)


def _layer_norm(z, g, b):
    mu = jnp.mean(z, axis=-1, keepdims=True)
    d = z - mu
    var = jnp.mean(d * d, axis=-1, keepdims=True)
    return d * lax.rsqrt(var + LN_EPS) * g + b


def _rows(x, c):
    return x[c * CHUNK:(c + 1) * CHUNK]


def _stack(parts):
    return parts[0] if len(parts) == 1 else jnp.concatenate(parts, axis=0)


def _mlstm_tile(p_ref, y_ref, state_ref, m_ref, mhg_ref, fresh, ts, fill):
    L = CHUNK
    nch = ts // L
    row_i = lax.broadcasted_iota(jnp.int32, (L, L), 0)
    col_i = lax.broadcasted_iota(jnp.int32, (L, L), 1)
    tri = (col_i <= row_i).astype(BF16)
    causal = _stack([col_i <= row_i] * nch)
    ones = jnp.ones((ts, LANES), BF16)

    def per_chunk(vals):
        return _stack([jnp.broadcast_to(vals[c], (L, LANES)) for c in range(nch)])

    ig = p_ref[:, pl.ds(GATE_I, LANES)]
    fg = p_ref[:, pl.ds(GATE_F, LANES)]
    logf = jnp.minimum(fg, 0.0) - jnp.log1p(jnp.exp(-jnp.abs(fg)))
    lf_hi, lf_lo = _split_bf16(logf)
    a = _stack([_dot(tri, _rows(lf_hi, c)) + _dot(tri, _rows(lf_lo, c)) for c in range(nch)])
    b_t = [(_rows(ig, c) - _rows(a, c)).T for c in range(nch)]
    fill()

    for h in range(N_HEADS):
        c0 = h * HEAD_DIM
        q = p_ref[:, pl.ds(OFF_Q + c0, HEAD_DIM)]
        k = p_ref[:, pl.ds(OFF_K + c0, HEAD_DIM)] * (HEAD_DIM ** -0.5)
        v = p_ref[:, pl.ds(OFF_V + c0, HEAD_DIM)]
        o = p_ref[:, pl.ds(OFF_O + c0, HEAD_DIM)]

        a_b = jnp.broadcast_to(a[:, h:h + 1], (ts, LANES))
        ig_b = jnp.broadcast_to(ig[:, h:h + 1], (ts, LANES))
        g = [a_b[(c + 1) * L - 1:(c + 1) * L, :] for c in range(nch)]
        w_end = per_chunk(g) - a_b + ig_b
        m_loc = [jnp.max(_rows(w_end, c), axis=0, keepdims=True) for c in range(nch)]
        m = [jnp.where(fresh, 0.0, m_ref[h:h + 1, :])]
        for c in range(nch):
            m.append(jnp.maximum(g[c] + m[c], m_loc[c]))
        m_ref[h:h + 1, :] = m[nch]
        log_inter = a_b + per_chunk(m)
        e_end = jnp.exp(w_end - per_chunk(m_loc))

        b_rows = _stack([jnp.broadcast_to(b_t[c][h:h + 1, :], (L, L)) for c in range(nch)])
        log_d = jnp.where(causal, a_b + b_rows, -jnp.inf)
        m_out = jnp.maximum(log_inter, jnp.max(log_d, axis=-1, keepdims=True))
        dmat = jnp.exp(log_d - m_out)
        if h < 2:
            fill()
        qb = q.astype(BF16)
        kb = k.astype(BF16)
        v_aug = jnp.concatenate([v.astype(BF16), ones], axis=1)
        s_qk = (_stack([_dot_nt(_rows(qb, c), _rows(kb, c)) for c in range(nch)]) * dmat).astype(BF16)
        intra = _stack([_dot(_rows(s_qk, c), _rows(v_aug, c)) for c in range(nch)])
        k_e = k * e_end
        s_loc = [_dot(_rows(k_e, c).T.astype(BF16), _rows(v_aug, c)) for c in range(nch)]
        state = [jnp.where(fresh, 0.0, state_ref[h])]
        for c in range(nch):
            keep = jnp.exp(g[c] + m[c] - m[c + 1])
            take = jnp.exp(m_loc[c] - m[c + 1])
            state.append(jnp.concatenate([keep, keep], axis=1) * state[c]
                         + jnp.concatenate([take, take], axis=1) * s_loc[c])
        state_ref[h] = state[nch]
        inter = _stack([_dot(_rows(qb, c), state[c].astype(BF16)) for c in range(nch)])
        e_inter = jnp.exp(log_inter - m_out)
        num = intra[:, :HEAD_DIM] + e_inter * inter[:, :HEAD_DIM]
        den = intra[:, HEAD_DIM:] + e_inter * inter[:, HEAD_DIM:]
        hh = num / jnp.maximum(jnp.abs(den), jnp.exp(-m_out))

        mu = jnp.mean(hh, axis=-1, keepdims=True)
        dh = hh - mu
        var = jnp.mean(dh * dh, axis=-1, keepdims=True)
        hn = dh * lax.rsqrt(var + LN_EPS) * mhg_ref[:, pl.ds(c0, HEAD_DIM)]
        y_ref[:, pl.ds(c0, HEAD_DIM)] = (jax.nn.sigmoid(o) * hn).astype(y_ref.dtype)
        fill()


def _route(logits_t, ts, fill):
    lg = [logits_t[e:e + 1, :] for e in range(N_EXPERTS)]
    mx = functools.reduce(jnp.maximum, lg)
    ex = [jnp.exp(v - mx) for v in lg]
    tot = functools.reduce(lambda a, b: a + b, ex)
    p = [v / tot for v in ex]
    fill()
    scores = []
    for g in range(N_GROUPS):
        p0, p1, p2, p3 = p[g * EXPERTS_PER_GROUP:(g + 1) * EXPERTS_PER_GROUP]
        hi01, lo01 = jnp.maximum(p0, p1), jnp.minimum(p0, p1)
        hi23, lo23 = jnp.maximum(p2, p3), jnp.minimum(p2, p3)
        top1 = jnp.maximum(hi01, hi23)
        top2 = jnp.maximum(jnp.minimum(hi01, hi23), jnp.maximum(lo01, lo23))
        scores.append(top1 + top2)
    grp = jnp.zeros((1, ts), jnp.int32)
    best = scores[0]
    for g in range(1, N_GROUPS):
        upd = scores[g] > best
        best = jnp.where(upd, scores[g], best)
        grp = jnp.where(upd, g, grp)
    fill()
    sel = []
    for j in range(EXPERTS_PER_GROUP):
        v = p[j]
        for g in range(1, N_GROUPS):
            v = jnp.where(grp == g, p[g * EXPERTS_PER_GROUP + j], v)
        sel.append(v)
    i1 = jnp.zeros((1, ts), jnp.int32)
    v1 = sel[0]
    for j in range(1, EXPERTS_PER_GROUP):
        upd = sel[j] > v1
        v1 = jnp.where(upd, sel[j], v1)
        i1 = jnp.where(upd, j, i1)
    fill()
    i2 = jnp.zeros((1, ts), jnp.int32)
    v2 = jnp.full((1, ts), -jnp.inf, F32)
    for j in range(EXPERTS_PER_GROUP):
        cand = jnp.where(i1 == j, -jnp.inf, sel[j])
        upd = cand > v2
        v2 = jnp.where(upd, cand, v2)
        i2 = jnp.where(upd, j, i2)
    e0 = grp * EXPERTS_PER_GROUP + i1
    e1 = grp * EXPERTS_PER_GROUP + i2
    s12 = v1 + v2
    return e0, e1, v1 / s12, v2 / s12


def _in_proj_pieces(x, w_ref, b_ref, p_ref):
    xb = x.astype(BF16)

    def piece(j):
        cols = pl.ds(j * PIECE_COLS, PIECE_COLS)
        p_ref[:, cols] = _dot(xb, w_ref[:, cols]) + b_ref[:, cols]

    return [functools.partial(piece, j) for j in range(W_ALL // PIECE_COLS)]


def _in_proj(x, w_ref, b_ref, p_ref):
    for piece in _in_proj_pieces(x, w_ref, b_ref, p_ref):
        piece()


class _Filler:
    def __init__(self, thunks):
        self._thunks = list(thunks)

    def __call__(self, count=1):
        for _ in range(count):
            if self._thunks:
                self._thunks.pop(0)()

    def drain(self):
        self(len(self._thunks))


def _mix_tile(p_ref, x, r0, fresh, first_tile, refs, ts, fill):
    (cw_ref, mhg_ref, wo_ref, lng_ref, lnb_ref, rw_ref, rb_ref,
     xo_ref, xp_ref, eidx_ref, gate_ref, rank_ref, cnt_ref,
     y_ref, state_ref, m_ref, ucarry_ref, cnt_scr) = refs
    _mlstm_tile(p_ref, y_ref, state_ref, m_ref, mhg_ref, fresh, ts, fill)

    gate_b = p_ref[:, pl.ds(W_M, D_CONV)]
    u = p_ref[:, pl.ds(W_M + D_CONV, D_CONV)] * p_ref[:, pl.ds(W_M + 2 * D_CONV, D_CONV)]
    rid = lax.broadcasted_iota(jnp.int32, (ts, D_CONV), 0)
    prev1 = jnp.where(fresh, 0.0, ucarry_ref[7:8, :])
    prev2 = jnp.where(fresh, 0.0, ucarry_ref[6:7, :])
    u1 = jnp.where(rid == 0, prev1, pltpu.roll(u, 1, 0))
    u2 = jnp.where(rid == 0, prev2, jnp.where(rid == 1, prev1, pltpu.roll(u, 2, 0)))
    conv = cw_ref[0:1, :] * u2 + cw_ref[1:2, :] * u1 + cw_ref[2:3, :] * u
    y_ref[:, pl.ds(D_MLSTM, D_CONV)] = (gate_b * conv).astype(y_ref.dtype)
    ucarry_ref[...] = u[ts - 8:ts, :]
    fill()

    mix = _dot(y_ref[...], wo_ref[...])
    x1 = _layer_norm(ALPHA * x + mix, lng_ref[...], lnb_ref[...])
    xo_ref[pl.ds(r0, ts), :] = x1
    xp_ref[pl.ds(r0, ts), :] = _pack_bf16(x1)
    fill()

    logits_t = _dot_nt(rw_ref[...], x1.astype(BF16)) + rb_ref[...]
    e0, e1, g0, g1 = _route(logits_t, ts, fill)
    eidx_ref[0:1, pl.ds(r0, ts)] = e0
    eidx_ref[1:2, pl.ds(r0, ts)] = e1
    krow = lax.broadcasted_iota(jnp.int32, (LANES, ts), 0)
    gate_ref[pl.ds(r0, ts), :] = jnp.where(krow == 0, g0, jnp.where(krow == 1, g1, 0.0)).T
    fill()

    eid = lax.broadcasted_iota(jnp.int32, (N_EXPERTS, ts), 0)
    hit0 = eid == e0
    hit1 = eid == e1
    onehot = hit0.astype(F32) + hit1.astype(F32)
    t_r = lax.broadcasted_iota(jnp.int32, (ts, ts), 0)
    t_c = lax.broadcasted_iota(jnp.int32, (ts, ts), 1)
    upper = (t_r <= t_c).astype(BF16)
    csum = _dot(onehot.astype(BF16), upper)
    seen = jnp.where(first_tile, 0.0, cnt_scr[...])
    before = csum - onehot + seen[:, 0:1]
    rank_ref[0:1, pl.ds(r0, ts)] = jnp.sum(jnp.where(hit0, before, 0.0), axis=0, keepdims=True).astype(jnp.int32)
    rank_ref[1:2, pl.ds(r0, ts)] = jnp.sum(jnp.where(hit1, before, 0.0), axis=0, keepdims=True).astype(jnp.int32)
    seen = seen + csum[:, ts - 1:ts]
    cnt_scr[...] = seen
    cnt_ref[...] = seen.astype(jnp.int32)


def _mixer_kernel(x_ref, xn_ref, w_ref, b_ref, cw_ref, mhg_ref, wo_ref, lng_ref, lnb_ref,
                  rw_ref, rb_ref,
                  xo_ref, xp_ref, eidx_ref, gate_ref, rank_ref, cnt_ref,
                  pa_ref, pb_ref, y_ref, state_ref, m_ref, ucarry_ref, cnt_scr, *, ts, seq):
    step = pl.program_id(0)
    refs = (cw_ref, mhg_ref, wo_ref, lng_ref, lnb_ref, rw_ref, rb_ref,
            xo_ref, xp_ref, eidx_ref, gate_ref, rank_ref, cnt_ref,
            y_ref, state_ref, m_ref, ucarry_ref, cnt_scr)

    @pl.when(step == 0)
    def _():
        _in_proj(x_ref[0:ts, :], w_ref, b_ref, pa_ref)

    tile_a = 2 * step
    fill = _Filler(_in_proj_pieces(x_ref[ts:2 * ts, :], w_ref, b_ref, pb_ref))
    _mix_tile(pa_ref, x_ref[0:ts, :], 0, (tile_a * ts) % seq == 0, tile_a == 0, refs, ts, fill)
    fill.drain()
    fill = _Filler(_in_proj_pieces(xn_ref[...], w_ref, b_ref, pa_ref))
    _mix_tile(pb_ref, x_ref[ts:2 * ts, :], ts, ((tile_a + 1) * ts) % seq == 0, False, refs, ts, fill)
    fill.drain()


def _const_spec(shape):
    nd = len(shape)
    return pl.BlockSpec(shape, lambda *_: (0,) * nd, pipeline_mode=pl.Buffered(1))


def _mixer_call(x_flat, seq, w_all, b_all, conv_w, mh_g, w_out, ln_g, ln_b, rw, rb):
    n, d = x_flat.shape
    ts = min(MIX_ROWS, seq // 2)
    n_steps = n // (2 * ts)
    tok_spec = pl.BlockSpec((TOP_K, 2 * ts), lambda i: (0, i))
    return pl.pallas_call(
        functools.partial(_mixer_kernel, ts=ts, seq=seq),
        grid=(n_steps,),
        in_specs=[
            pl.BlockSpec((2 * ts, d), lambda i: (i, 0)),
            pl.BlockSpec((ts, d), lambda i: (2 * jnp.minimum(i + 1, n_steps - 1), 0)),
            _const_spec(w_all.shape), _const_spec(b_all.shape), _const_spec(conv_w.shape),
            _const_spec(mh_g.shape), _const_spec(w_out.shape), _const_spec(ln_g.shape),
            _const_spec(ln_b.shape), _const_spec(rw.shape), _const_spec(rb.shape),
        ],
        out_specs=[
            pl.BlockSpec((2 * ts, d), lambda i: (i, 0)),
            pl.BlockSpec((2 * ts, d // 2), lambda i: (i, 0)),
            tok_spec, pl.BlockSpec((2 * ts, LANES), lambda i: (i, 0)), tok_spec,
            pl.BlockSpec((N_EXPERTS, LANES), lambda i: (0, 0)),
        ],
        out_shape=[
            jax.ShapeDtypeStruct((n, d), F32),
            jax.ShapeDtypeStruct((n, d // 2), jnp.uint32),
            jax.ShapeDtypeStruct((TOP_K, n), jnp.int32),
            jax.ShapeDtypeStruct((n, LANES), F32),
            jax.ShapeDtypeStruct((TOP_K, n), jnp.int32),
            jax.ShapeDtypeStruct((N_EXPERTS, LANES), jnp.int32),
        ],
        scratch_shapes=[
            pltpu.VMEM((ts, W_ALL), F32),
            pltpu.VMEM((ts, W_ALL), F32),
            pltpu.VMEM((ts, D_MODEL), BF16),
            pltpu.VMEM((N_HEADS, HEAD_DIM, 2 * HEAD_DIM), F32),
            pltpu.VMEM((8, LANES), F32),
            pltpu.VMEM((8, D_CONV), F32),
            pltpu.VMEM((N_EXPERTS, LANES), F32),
        ],
        compiler_params=pltpu.CompilerParams(
            dimension_semantics=("arbitrary",), vmem_limit_bytes=VMEM_LIMIT),
        name="mixer",
    )(x_flat, x_flat, w_all, b_all, conv_w, mh_g, w_out, ln_g, ln_b, rw, rb)


def _sc_mesh():
    return plsc.VectorSubcoreMesh(core_axis_name="core", subcore_axis_name="subcore")


def _sc_scatter_pair(x_flat, dest, n_slots):
    n, d = x_flat.shape
    windows = [dest[k].reshape(n // SC_WINDOW, 1, SC_WINDOW) for k in range(TOP_K)]

    @functools.partial(
        pl.kernel, out_type=jax.ShapeDtypeStruct((n_slots, d), x_flat.dtype), mesh=_sc_mesh(),
        scratch_types=[])
    def scatter_kernel(x_hbm, *rest):
        idx_hbm, o_hbm = rest[:TOP_K], rest[TOP_K]

        def body(x_vmem, *idx_vmem):
            for i_vmem in idx_vmem:
                pltpu.sync_copy(x_vmem, o_hbm.at[i_vmem.at[0, 0]])

        pltpu.emit_pipeline(
            body,
            grid=(n // SC_WINDOW,),
            in_specs=[pl.BlockSpec((SC_WINDOW, d), lambda i: (i, 0))]
            + [pl.BlockSpec((1, 1, SC_WINDOW), lambda i: (i, 0, 0))] * TOP_K,
            out_specs=[],
            core_axis_name=("core", "subcore"),
            dimension_semantics=(pltpu.PARALLEL,),
        )(x_hbm, *idx_hbm)

    return scatter_kernel(x_flat, *windows)


def _expert_kernel(be_ref, nv_ref, br_ref, buf_ref, wg_ref, wu_ref, wd_ref, y_ref, wgu_scr, wd_scr):
    step = pl.program_id(0)
    used = step < nv_ref[0]

    @pl.when(jnp.logical_and(used, jnp.logical_or(step == 0, be_ref[step] != be_ref[jnp.maximum(step - 1, 0)])))
    def _():
        wgu_scr[:, :D_FF] = wg_ref[0, 0].astype(BF16)
        wgu_scr[:, D_FF:] = wu_ref[0, 0].astype(BF16)
        wd_scr[...] = wd_ref[0, 0].astype(BF16)

    @pl.when(used)
    def _():
        rid = lax.broadcasted_iota(jnp.int32, (MOE_BLOCK, 1), 0)
        words = jnp.where(rid < br_ref[step], buf_ref[...], jnp.uint32(0))
        xb = _unpack_bf16(words).astype(BF16)
        h = _dot(xb, wgu_scr[...])
        hdn = jax.nn.silu(h[:, :D_FF]) * h[:, D_FF:]
        y_ref[...] = _pack_bf16(_dot(hdn.astype(BF16), wd_scr[...]))

    @pl.when(jnp.logical_not(used))
    def _():
        y_ref[...] = jnp.zeros_like(y_ref)


def _expert_call(layer, block_expert, n_valid, block_rows, buf, w_gate, w_up, w_down):
    n_slots, dw = buf.shape
    d = 2 * dw
    n_blocks = n_slots // MOE_BLOCK

    def row_map(i, be, nv, br):
        return (jnp.minimum(i, nv[0] - 1), 0)

    def w_map(i, be, nv, br):
        return (layer, be[jnp.minimum(i, nv[0] - 1)], 0, 0)

    return pl.pallas_call(
        _expert_kernel,
        grid_spec=pltpu.PrefetchScalarGridSpec(
            num_scalar_prefetch=3,
            grid=(n_blocks,),
            in_specs=[
                pl.BlockSpec((MOE_BLOCK, dw), row_map),
                pl.BlockSpec((1, 1, d, D_FF), w_map),
                pl.BlockSpec((1, 1, d, D_FF), w_map),
                pl.BlockSpec((1, 1, D_FF, d), w_map),
            ],
            out_specs=pl.BlockSpec((MOE_BLOCK, dw), lambda i, be, nv, br: (i, 0)),
            scratch_shapes=[pltpu.VMEM((d, 2 * D_FF), BF16), pltpu.VMEM((D_FF, d), BF16)],
        ),
        out_shape=jax.ShapeDtypeStruct((n_slots, dw), jnp.uint32),
        compiler_params=pltpu.CompilerParams(
            dimension_semantics=("arbitrary",), vmem_limit_bytes=VMEM_LIMIT),
        name="experts",
    )(block_expert, n_valid, block_rows, buf, w_gate, w_up, w_down)


def _sc_gather(table, idx):
    m = idx.shape[0]
    d = table.shape[1]

    @functools.partial(
        pl.kernel, out_type=jax.ShapeDtypeStruct((m, d), table.dtype), mesh=_sc_mesh(),
        scratch_types=[])
    def gather_kernel(t_hbm, i_hbm, o_hbm):
        def body(i_vmem, o_vmem):
            pltpu.sync_copy(t_hbm.at[i_vmem.at[0, 0]], o_vmem)

        pltpu.emit_pipeline(
            body,
            grid=(m // SC_WINDOW,),
            in_specs=[pl.BlockSpec((1, 1, SC_WINDOW), lambda i: (i, 0, 0))],
            out_specs=[pl.BlockSpec((SC_WINDOW, d), lambda i: (i, 0))],
            core_axis_name=("core", "subcore"),
            dimension_semantics=(pltpu.PARALLEL,),
        )(i_hbm, o_hbm)

    return gather_kernel(table, idx.reshape(m // SC_WINDOW, 1, SC_WINDOW))


def _combine_kernel(x_ref, y0_ref, y1_ref, gate_ref, lng_ref, lnb_ref, o_ref):
    ffn = _unpack_bf16(y0_ref[...]) * gate_ref[:, 0:1] + _unpack_bf16(y1_ref[...]) * gate_ref[:, 1:2]
    o_ref[...] = _layer_norm(ALPHA * x_ref[...] + ffn, lng_ref[...], lnb_ref[...])


def _combine_call(x_flat, dest, gates_t, y_buf, ln_g, ln_b):
    n, d = x_flat.shape
    rows = min(COMBINE_ROWS, n)
    nb = n // rows
    yg = _sc_gather(y_buf, dest.reshape(TOP_K * n))
    return pl.pallas_call(
        _combine_kernel,
        grid=(nb,),
        in_specs=[
            pl.BlockSpec((rows, d), lambda i: (i, 0)),
            pl.BlockSpec((rows, d // 2), lambda i: (i, 0)),
            pl.BlockSpec((rows, d // 2), lambda i: (i + nb, 0)),
            pl.BlockSpec((rows, LANES), lambda i: (i, 0)),
            _const_spec(ln_g.shape), _const_spec(ln_b.shape),
        ],
        out_specs=pl.BlockSpec((rows, d), lambda i: (i, 0)),
        out_shape=jax.ShapeDtypeStruct((n, d), F32),
        compiler_params=pltpu.CompilerParams(
            dimension_semantics=("arbitrary",), vmem_limit_bytes=VMEM_LIMIT),
        name="combine",
    )(x_flat, yg, yg, gates_t, ln_g, ln_b)


def _slot_plan(counts, eidx, rank, n_blocks):
    padded = (counts + MOE_BLOCK - 1) // MOE_BLOCK * MOE_BLOCK
    pad_end = jnp.cumsum(padded)
    pad_start = pad_end - padded
    dest = rank
    for e in range(N_EXPERTS):
        dest = dest + jnp.where(eidx == e, pad_start[e], 0)
    block_row0 = jnp.arange(n_blocks, dtype=jnp.int32) * MOE_BLOCK
    block_expert = jnp.minimum(
        jnp.sum(pad_end[None, :] <= block_row0[:, None], axis=1), N_EXPERTS - 1).astype(jnp.int32)
    n_valid = (pad_end[-1:] // MOE_BLOCK).astype(jnp.int32)
    block_rows = jnp.clip(
        (pad_start + counts)[block_expert] - block_row0, 0, MOE_BLOCK).astype(jnp.int32)
    return dest.astype(jnp.int32), block_expert, n_valid, block_rows


def kernel(x, w_in, b_in, conv_w, mh_norm_g, w_out, ln_mix_g, ln_mix_b, router_w, router_b,
           w_gate, w_up, w_down, ln_moe_g, ln_moe_b):
    bsz, seq, d = x.shape
    n = bsz * seq

    gate_pad = jnp.zeros(w_in.shape[:2] + (LANES - N_HEADS,), w_in.dtype)
    w_all = jnp.concatenate(
        [w_in[..., OFF_Q:OFF_I], w_in[..., OFF_B:P_IN], w_in[..., OFF_I:OFF_F], gate_pad,
         w_in[..., OFF_F:OFF_B], gate_pad], axis=-1).astype(BF16)
    b_all = jnp.concatenate(
        [b_in[..., OFF_Q:OFF_I], b_in[..., OFF_B:P_IN], b_in[..., OFF_I:OFF_F], gate_pad[:, 0, :],
         b_in[..., OFF_F:OFF_B], gate_pad[:, 0, :]], axis=-1).astype(F32)[:, None, :]
    w_out_b = w_out.astype(BF16)
    rw = router_w.T.astype(BF16)
    rb = router_b.astype(F32)[:, None]

    n_slots = n * TOP_K + N_EXPERTS * MOE_BLOCK
    n_blocks = n_slots // MOE_BLOCK
    xf = x.reshape(n, d)
    for l in range(DEPTH):
        x1, x1p, eidx, gates, rank, cnt = _mixer_call(
            xf, seq, w_all[l], b_all[l], conv_w[l], mh_norm_g[l][None, :], w_out_b[l],
            ln_mix_g[l][None, :], ln_mix_b[l][None, :], rw, rb)
        dest, block_expert, n_valid, block_rows = _slot_plan(cnt[:, 0], eidx, rank, n_blocks)
        buf = _sc_scatter_pair(x1p, dest, n_slots)
        y_buf = _expert_call(l, block_expert, n_valid, block_rows, buf, w_gate, w_up, w_down)
        xf = _combine_call(x1, dest, gates, y_buf, ln_moe_g[l][None, :], ln_moe_b[l][None, :])
    return xf.reshape(bsz, seq, d)
```
